```python
import jax, jax.numpy as jnp
from jax import lax
import numpy as np

D_MODEL = 1024
BATCH = 2
SEQ = 8192
DEPTH = 1

CHUNK = 64
EPS = 1e-6

SSD_HEADS = 16
SSD_HEAD_DIM = 64
SSD_INNER = SSD_HEADS * SSD_HEAD_DIM
SSD_GROUPS = 4
SSD_HPG = SSD_HEADS // SSD_GROUPS
SSD_STATE = 128
SSD_CONV = 4
SSD_CONV_CH = SSD_INNER + 2 * SSD_GROUPS * SSD_STATE

HG_HEADS = 8
HG_KDIM = 128
HG_VDIM = 128
HG_K = HG_HEADS * HG_KDIM
HG_V = HG_HEADS * HG_VDIM

MIX_WIDTH = SSD_INNER + HG_V
IN_SIZES = (SSD_INNER, SSD_CONV_CH, SSD_HEADS, HG_K, HG_K, HG_V, HG_V)
IN_COLS = SSD_INNER + SSD_CONV_CH + SSD_HEADS + 2 * HG_K + 2 * HG_V

PEER_HEADS = 8
PEER_KEYS = 128
PEER_EXPERTS = PEER_KEYS * PEER_KEYS
PEER_TOPK = 16
PEER_QDIM = 256
PEER_HALF = PEER_QDIM // 2
PEER_BLOCK = 128

kernel_name = "hymba_ssd_hgrn2_peer_block"


def rmsnorm(x, w):
    xf = x.astype(jnp.float32)
    r = xf * lax.rsqrt(jnp.mean(xf * xf, axis=-1, keepdims=True) + EPS)
    return (r * w.astype(jnp.float32)).astype(x.dtype)


def causal_dwconv(x, w, b):
    k, c = w.shape
    y = lax.conv_general_dilated(x, w[:, None, :].astype(x.dtype), window_strides=(1,),
                                 padding=[(k - 1, 0)], dimension_numbers=('NWC', 'WIO', 'NWC'),
                                 feature_group_count=c)
    return y + b.astype(x.dtype)


def segsum_exp(a):
    n = a.shape[-1]
    cs = jnp.cumsum(a, axis=-1)
    diff = cs[..., :, None] - cs[..., None, :]
    mask = jnp.tril(jnp.ones((n, n), dtype=bool))
    return jnp.where(mask, jnp.exp(jnp.where(mask, diff, 0.0)), 0.0)


def ssd_mixer(z, xbc, dt_raw, conv_w, conv_b, dt_bias, a_log, d_skip, norm_w):
    bsz, seqlen, _ = z.shape
    nc = seqlen // CHUNK
    g, r, p, n = SSD_GROUPS, SSD_HPG, SSD_HEAD_DIM, SSD_STATE
    xbc = jax.nn.silu(causal_dwconv(xbc, conv_w, conv_b))
    xs = xbc[..., :SSD_INNER].reshape(bsz, nc, CHUNK, g, r, p)
    bm = xbc[..., SSD_INNER:SSD_INNER + g * n].reshape(bsz, nc, CHUNK, g, n)
    cm = xbc[..., SSD_INNER + g * n:].reshape(bsz, nc, CHUNK, g, n)
    dt = jax.nn.softplus(dt_raw.astype(jnp.float32) + dt_bias.astype(jnp.float32))
    a_head = -jnp.exp(a_log.astype(jnp.float32))
    dt = dt.reshape(bsz, nc, CHUNK, g, r)
    a = jnp.moveaxis(dt * a_head.reshape(g, r), 2, -1)
    xdt = xs * dt[..., None]
    a_cum = jnp.cumsum(a, axis=-1)
    decay = segsum_exp(a)
    cb = jnp.einsum('bclgn,bcsgn->bcgls', cm, bm)
    y_diag = jnp.einsum('bcgrls,bcsgrp->bclgrp', cb[:, :, :, None] * decay, xdt)
    to_end = jnp.moveaxis(jnp.exp(a_cum[..., -1:] - a_cum), -1, 2)
    states = jnp.einsum('bclgn,bclgrp->bcgrpn', bm, xdt * to_end[..., None])
    chunk_decay = jnp.exp(a_cum[..., -1])

    def step(s, inp):
        st, dec = inp
        return s * dec[..., None, None] + st, s

    s0 = jnp.zeros((bsz, g, r, p, n), states.dtype)
    _, prev = lax.scan(step, s0, (jnp.moveaxis(states, 1, 0), jnp.moveaxis(chunk_decay, 1, 0)))
    prev = jnp.moveaxis(prev, 0, 1)
    from_start = jnp.moveaxis(jnp.exp(a_cum), -1, 2)
    y_off = jnp.einsum('bclgn,bcgrpn->bclgrp', cm, prev) * from_start[..., None]
    y = y_diag + y_off + xs * d_skip.reshape(g, r)[:, :, None]
    y = y.reshape(bsz, seqlen, SSD_INNER)
    return rmsnorm(y * jax.nn.silu(z.astype(y.dtype)), norm_w)


def hgrn2_mixer(q_raw, f_raw, i_raw, g_raw, lb, norm_w):
    bsz, seqlen, _ = q_raw.shape
    nc = seqlen // CHUNK
    f = lb + (1.0 - lb) * jax.nn.sigmoid(f_raw.astype(jnp.float32))
    logf = jnp.log(f)
    k = 1.0 - f
    q = jax.nn.silu(q_raw.astype(jnp.float32))
    v = i_raw.astype(jnp.float32)
    mask = jnp.tril(jnp.ones((CHUNK, CHUNK), dtype=bool))[None, :, :, None, None]

    def chunks(t, d):
        return jnp.swapaxes(t.reshape(bsz, nc, CHUNK, HG_HEADS, d), 0, 1)

    def step(s, inp):
        qc, kc, vc, gc = inp
        gcum = jnp.cumsum(gc, axis=1)
        o_inter = jnp.einsum('blhk,bhkv->blhv', qc * jnp.exp(gcum), s)
        diff = gcum[:, :, None] - gcum[:, None, :]
        dec = jnp.where(mask, jnp.exp(jnp.where(mask, diff, 0.0)), 0.0)
        att = jnp.sum(qc[:, :, None] * kc[:, None] * dec, axis=-1)
        o_intra = jnp.einsum('blsh,bshv->blhv', att, vc)
        g_end = gcum[:, -1]
        s_new = s * jnp.exp(g_end)[..., None] + jnp.einsum(
            'bshk,bshv->bhkv', kc * jnp.exp(g_end[:, None] - gcum), vc)
        return s_new, o_inter + o_intra

    s0 = jnp.zeros((bsz, HG_HEADS, HG_KDIM, HG_VDIM), jnp.float32)
    _, o = lax.scan(step, s0, (chunks(q, HG_KDIM), chunks(k, HG_KDIM),
                               chunks(v, HG_VDIM), chunks(logf, HG_KDIM)))
    o = jnp.swapaxes(o, 0, 1).reshape(bsz, seqlen, HG_HEADS, HG_VDIM)
    gate = jax.nn.silu(g_raw.astype(jnp.float32).reshape(bsz, seqlen, HG_HEADS, HG_VDIM))
    o = rmsnorm(o, norm_w.reshape(HG_HEADS, HG_VDIM)) * gate
    return o.reshape(bsz, seqlen, HG_V)


def peer_ffn(x, w_query, sub_keys, u_table, v_table):
    bsz, seqlen, dm = x.shape
    t = bsz * seqlen
    xt = x.reshape(t, dm)
    q = (xt @ w_query).reshape(t, PEER_HEADS, 2, PEER_HALF)
    s = jnp.einsum('thpd,hpnd->thpn', q, sub_keys).astype(jnp.float32)
    s_top, i_top = lax.top_k(s, PEER_TOPK)
    cand = (s_top[:, :, 0, :, None] + s_top[:, :, 1, None, :]).reshape(t, PEER_HEADS, PEER_TOPK * PEER_TOPK)
    c_top, c_idx = lax.top_k(cand, PEER_TOPK)
    idx1 = jnp.take_along_axis(i_top[:, :, 0], c_idx // PEER_TOPK, axis=-1)
    idx2 = jnp.take_along_axis(i_top[:, :, 1], c_idx % PEER_TOPK, axis=-1)
    expert = idx1 * PEER_KEYS + idx2
    gate = jax.nn.softmax(c_top, axis=-1)

    def block(args):
        xb, eb, gb = args
        u = u_table[eb]
        hid = jax.nn.gelu(jnp.einsum('td,thkd->thk', xb, u).astype(jnp.float32), approximate=False) * gb
        return jnp.einsum('thk,thkd->td', hid.astype(xb.dtype), v_table[eb])

    nb = t // PEER_BLOCK
    out = lax.map(block, (xt.reshape(nb, PEER_BLOCK, dm),
                          expert.reshape(nb, PEER_BLOCK, PEER_HEADS, PEER_TOPK),
                          gate.reshape(nb, PEER_BLOCK, PEER_HEADS, PEER_TOPK)))
    return out.reshape(bsz, seqlen, dm)


def setup_inputs(seed: int = 0) -> dict:
    key = jax.random.key(seed)
    ks = jax.random.split(key, 20)
    f32 = jnp.float32
    nrm = lambda k, shape, sc: jax.random.normal(k, shape, f32) * sc
    dt0 = jnp.exp(jax.random.uniform(ks[4], (DEPTH, SSD_HEADS), f32, np.log(1e-3), np.log(1e-1)))
    return {
        "x": nrm(ks[0], (BATCH, SEQ, D_MODEL), 1.0),
        "ln_mix_w": 1.0 + nrm(ks[1], (DEPTH, D_MODEL), 0.05),
        "w_in": nrm(ks[2], (DEPTH, D_MODEL, IN_COLS), D_MODEL ** -0.5),
        "conv_w": nrm(ks[3], (DEPTH, SSD_CONV, SSD_CONV_CH), SSD_CONV ** -0.5),
        "conv_b": nrm(ks[5], (DEPTH, SSD_CONV_CH), 0.02),
        "dt_bias": dt0 + jnp.log(-jnp.expm1(-dt0)),
        "a_log": jnp.log(jax.random.uniform(ks[6], (DEPTH, SSD_HEADS), f32, 1.0, 16.0)),
        "d_skip": 1.0 + nrm(ks[7], (DEPTH, SSD_HEADS), 0.1),
        "ssd_norm_w": 1.0 + nrm(ks[8], (DEPTH, SSD_INNER), 0.05),
        "lb_logits": 1.0 + nrm(ks[9], (DEPTH + 1, HG_K), 0.1),
        "hg_norm_w": 1.0 + nrm(ks[10], (DEPTH, HG_V), 0.05),
        "w_out": nrm(ks[11], (DEPTH, MIX_WIDTH, D_MODEL), MIX_WIDTH ** -0.5),
        "ln_ffn_w": 1.0 + nrm(ks[12], (DEPTH, D_MODEL), 0.05),
        "w_query": nrm(ks[13], (DEPTH, D_MODEL, PEER_HEADS * PEER_QDIM), D_MODEL ** -0.5),
        "sub_keys": nrm(ks[14], (DEPTH, PEER_HEADS, 2, PEER_KEYS, PEER_HALF), PEER_HALF ** -0.5),
        "u_table": nrm(ks[15], (DEPTH, PEER_EXPERTS, D_MODEL), D_MODEL ** -0.5),
        "v_table": nrm(ks[16], (DEPTH, PEER_EXPERTS, D_MODEL), 0.5),
        "ln_final_w": 1.0 + nrm(ks[17], (D_MODEL,), 0.05),
    }


def reference(x, ln_mix_w, w_in, conv_w, conv_b, dt_bias, a_log, d_skip, ssd_norm_w,
              lb_logits, hg_norm_w, w_out, ln_ffn_w, w_query, sub_keys, u_table, v_table,
              ln_final_w):
    split_at = [int(v) for v in np.cumsum(IN_SIZES)[:-1]]
    lb_all = jnp.cumsum(jax.nn.softmax(lb_logits.astype(jnp.float32), axis=0), axis=0)
    h = x
    for l in range(DEPTH):
        hn = rmsnorm(h, ln_mix_w[l])
        proj = hn @ w_in[l]
        z, xbc, dt_raw, q_raw, f_raw, i_raw, g_raw = jnp.split(proj, split_at, axis=-1)
        y_ssd = ssd_mixer(z, xbc, dt_raw, conv_w[l], conv_b[l], dt_bias[l], a_log[l],
                          d_skip[l], ssd_norm_w[l])
        y_hg = hgrn2_mixer(q_raw, f_raw, i_raw, g_raw, lb_all[l], hg_norm_w[l])
        mix = jnp.concatenate([y_ssd.astype(h.dtype), y_hg.astype(h.dtype)], axis=-1)
        h = h + mix @ w_out[l]
        h = h + peer_ffn(rmsnorm(h, ln_ffn_w[l]), w_query[l], sub_keys[l], u_table[l], v_table[l])
    return rmsnorm(h, ln_final_w)
```

```python
import functools

import numpy as np
import jax
import jax.numpy as jnp
from jax import lax
from jax.experimental import pallas as pl
from jax.experimental.pallas import tpu as pltpu

F32 = jnp.float32
BF16 = jnp.bfloat16
EPS = 1e-6

SSD_HEADS = 16
SSD_HEAD_DIM = 64
SSD_GROUPS = 4
SSD_STATE = 128
SSD_CONV = 4
HG_HEADS = 8
HG_DIM = 128
PEER_HEADS = 8
PEER_KEYS = 128
PEER_TOPK = 16

LANES = 128
SUBLANES = 8
VMEM_LIMIT = 56 * 1024 * 1024

CONV_PAD = SUBLANES


def _params(sem):
    return pltpu.CompilerParams(dimension_semantics=sem, vmem_limit_bytes=VMEM_LIMIT)


def _split3(x):
    hi = x.astype(BF16)
    r1 = x - hi.astype(F32)
    mid = r1.astype(BF16)
    lo = (r1 - mid.astype(F32)).astype(BF16)
    return hi, mid, lo


def _split2(x):
    hi = x.astype(BF16)
    lo = (x - hi.astype(F32)).astype(BF16)
    return hi, lo


def _dot(a, b):
    return jnp.dot(a, b, preferred_element_type=F32)


def _dot_nt(a, b):
    return lax.dot_general(a, b, (((1,), (1,)), ((), ())), preferred_element_type=F32)


def _dot_exact_rhs(m01, x):
    hi, mid, lo = _split3(x)
    return _dot(m01, hi) + _dot(m01, mid) + _dot(m01, lo)


def _dot_exact_lhs(x, m01):
    hi, mid, lo = _split3(x)
    return _dot(hi, m01) + _dot(mid, m01) + _dot(lo, m01)


def _silu(x):
    return x * jax.nn.sigmoid(x)


def _softplus(x):
    return jnp.maximum(x, 0.0) + jnp.log1p(jnp.exp(-jnp.abs(x)))


def _rmsnorm_rows(x, w):
    ms = jnp.mean(x * x, axis=-1, keepdims=True)
    return x * lax.rsqrt(ms + EPS) * w


def _inproj_kernel(col_slices, x_ref, lnw_ref, w_ref, *out_refs):
    hn = _rmsnorm_rows(x_ref[...], lnw_ref[...]).astype(BF16)
    for ref, (off, width) in zip(out_refs, col_slices):
        ref[...] = _dot(hn, w_ref[:, off:off + width]).astype(ref.dtype)


def _inproj_call(x2, lnw, w_pad, col_slices, tm):
    t, d = x2.shape
    out_shape = [jax.ShapeDtypeStruct((t, width), F32) for _, width in col_slices]
    out_specs = [pl.BlockSpec((tm, width), lambda i: (i, 0)) for _, width in col_slices]
    return pl.pallas_call(
        functools.partial(_inproj_kernel, col_slices),
        grid=(t // tm,),
        in_specs=[
            pl.BlockSpec((tm, d), lambda i: (i, 0)),
            pl.BlockSpec((1, d), lambda i: (0, 0)),
            pl.BlockSpec(w_pad.shape, lambda i: (0, 0), pipeline_mode=pl.Buffered(1)),
        ],
        out_specs=out_specs,
        out_shape=out_shape,
        compiler_params=_params(("parallel",)),
        name="inproj",
    )(x2, lnw, w_pad)


def _ssd_kernel(lc, z_ref, xbc_ref, dt_ref, convw_ref, convb_ref, dtb_ref, ahead_ref, dskip_ref,
                normw_ref, expand_ref, y_ref, xpad_ref, state_ref, ybuf_ref):
    inner = SSD_HEADS * SSD_HEAD_DIM
    gn = SSD_GROUPS * SSD_STATE
    pair_w = 2 * SSD_HEAD_DIM

    @pl.when(pl.program_id(1) == 0)
    def _():
        xpad_ref[0:CONV_PAD, :] = jnp.zeros((CONV_PAD, xpad_ref.shape[1]), F32)
        state_ref[...] = jnp.zeros(state_ref.shape, F32)

    xpad_ref[CONV_PAD:CONV_PAD + lc, :] = xbc_ref[...]
    conv = convb_ref[...]
    for k in range(SSD_CONV):
        start = CONV_PAD - (SSD_CONV - 1) + k
        conv = conv + convw_ref[k:k + 1, :] * xpad_ref[start:start + lc, :]
    xpad_ref[0:CONV_PAD, :] = xpad_ref[lc:lc + CONV_PAD, :]
    xbc = _silu(conv)
    xs = xbc[:, :inner]

    dt = _softplus(dt_ref[...] + dtb_ref[...])
    a = dt * ahead_ref[...]

    row = lax.broadcasted_iota(jnp.int32, (lc, lc), 0)
    col = lax.broadcasted_iota(jnp.int32, (lc, lc), 1)
    tril = row >= col
    tri_bf = tril.astype(BF16)
    acum = _dot_exact_rhs(tri_bf, a)
    acum_t = acum.T
    a_last = acum[lc - 1:lc, :]

    expand = expand_ref[...]
    dt_x = _dot_exact_lhs(dt, expand)
    from_start_x = _dot_exact_lhs(jnp.exp(acum), expand)
    to_end_x = _dot_exact_lhs(jnp.exp(a_last - acum), expand)
    chunk_decay = jnp.exp(a_last)
    xdt = xs * dt_x
    xw = xdt * to_end_x

    lane = lax.broadcasted_iota(jnp.int32, (lc, pair_w), 1)
    first_half = lane < SSD_HEAD_DIM

    heads_per_group = SSD_HEADS // SSD_GROUPS
    for g in range(SSD_GROUPS):
        bm = xbc[:, inner + g * SSD_STATE: inner + (g + 1) * SSD_STATE].astype(BF16)
        cm = xbc[:, inner + gn + g * SSD_STATE: inner + gn + (g + 1) * SSD_STATE].astype(BF16)
        cb = _dot_nt(cm, bm)
        for pr in range(heads_per_group // 2):
            pidx = g * (heads_per_group // 2) + pr
            lo, hi = pidx * pair_w, (pidx + 1) * pair_w
            xdt_p = xdt[:, lo:hi]
            y_pair = None
            for sub in range(2):
                h = 2 * pidx + sub
                diff = acum[:, h:h + 1] - acum_t[h:h + 1, :]
                decay = jnp.where(tril, jnp.exp(jnp.where(tril, diff, 0.0)), 0.0)
                m_h = (cb * decay).astype(BF16)
                sel = first_half if sub == 0 else jnp.logical_not(first_half)
                contrib = _dot(m_h, jnp.where(sel, xdt_p, 0.0).astype(BF16))
                y_pair = contrib if y_pair is None else y_pair + contrib
            st = state_ref[lo:hi, :]
            y_off = _dot_nt(cm, st.astype(BF16)) * from_start_x[:, lo:hi]
            ybuf_ref[:, lo:hi] = y_pair + y_off + xs[:, lo:hi] * dskip_ref[:, lo:hi]
            r_idx = lax.broadcasted_iota(jnp.int32, (pair_w, SSD_STATE), 0)
            dec = jnp.where(r_idx < SSD_HEAD_DIM, chunk_decay[:, 2 * pidx:2 * pidx + 1],
                            chunk_decay[:, 2 * pidx + 1:2 * pidx + 2])
            state_ref[lo:hi, :] = st * dec + _dot(xw[:, lo:hi].T.astype(BF16), bm)

    y = ybuf_ref[...] * _silu(z_ref[...])
    y_ref[...] = _rmsnorm_rows(y, normw_ref[...]).astype(y_ref.dtype)


def _ssd_call(z, xbc, dtp, convw, convb, dtb, ahead, dskip_x, normw, expand, bsz, lc):
    t, inner = z.shape
    cw = xbc.shape[1]
    nc = t // bsz // lc
    tok = lambda b, c: (b * nc + c, 0)
    const = lambda b, c: (0, 0)
    return pl.pallas_call(
        functools.partial(_ssd_kernel, lc),
        grid=(bsz, nc),
        in_specs=[
            pl.BlockSpec((lc, inner), tok),
            pl.BlockSpec((lc, cw), tok),
            pl.BlockSpec((lc, LANES), tok),
            pl.BlockSpec(convw.shape, const),
            pl.BlockSpec(convb.shape, const),
            pl.BlockSpec(dtb.shape, const),
            pl.BlockSpec(ahead.shape, const),
            pl.BlockSpec(dskip_x.shape, const),
            pl.BlockSpec(normw.shape, const),
            pl.BlockSpec(expand.shape, const),
        ],
        out_specs=pl.BlockSpec((lc, inner), tok),
        out_shape=jax.ShapeDtypeStruct((t, inner), BF16),
        scratch_shapes=[
            pltpu.VMEM((lc + CONV_PAD, cw), F32),
            pltpu.VMEM((inner, SSD_STATE), F32),
            pltpu.VMEM((lc, inner), F32),
        ],
        compiler_params=_params(("parallel", "arbitrary")),
        name="ssd",
    )(z, xbc, dtp, convw, convb, dtb, ahead, dskip_x, normw, expand)


def _hgrn2_kernel(lc, sb, q_ref, f_ref, i_ref, g_ref, lb_ref, normw_ref, y_ref, state_ref):
    nsb = lc // sb

    @pl.when(pl.program_id(1) == 0)
    def _():
        state_ref[...] = jnp.zeros(state_ref.shape, F32)

    lb = lb_ref[...]
    f = lb + (1.0 - lb) * jax.nn.sigmoid(f_ref[...])
    logf = jnp.log(f)
    kk = 1.0 - f
    q = _silu(q_ref[...])
    v = i_ref[...]
    v_bf = v.astype(BF16)

    row = lax.broadcasted_iota(jnp.int32, (lc, lc), 0)
    col = lax.broadcasted_iota(jnp.int32, (lc, lc), 1)
    tri_bf = (row >= col).astype(BF16)
    gcum = _dot_exact_rhs(tri_bf, logf)
    g_end = gcum[lc - 1:lc, :]
    k_end_bf = (kk * jnp.exp(g_end - gcum)).astype(BF16)
    state_decay = jnp.exp(g_end)

    qs_bf, q_inter_bf, k_ref_bf = [], [], []
    for ib in range(nsb):
        r0, r1 = ib * sb, (ib + 1) * sb
        if ib == 0:
            gref = jnp.zeros_like(g_end)
        else:
            gref = gcum[r0 - 1:r0, :]
        e_q = jnp.exp(gcum[r0:r1, :] - gref)
        qs = q[r0:r1, :] * e_q
        qs_bf.append(qs.astype(BF16))
        q_inter_bf.append((qs * jnp.exp(gref)).astype(BF16))
        k_ref_bf.append((kk[:r1, :] * jnp.exp(gref - gcum[:r1, :])).astype(BF16))

    for h in range(HG_HEADS):
        lo, hi = h * HG_DIM, (h + 1) * HG_DIM
        st_t = state_ref[lo:hi, :]
        st_bf = st_t.astype(BF16)
        v_h = v_bf[:, lo:hi]
        for ib in range(nsb):
            r0, r1 = ib * sb, (ib + 1) * sb
            att = _dot_nt(qs_bf[ib][:, lo:hi], k_ref_bf[ib][:, lo:hi])
            rr = lax.broadcasted_iota(jnp.int32, (sb, r1), 0) + r0
            cc = lax.broadcasted_iota(jnp.int32, (sb, r1), 1)
            att = jnp.where(rr >= cc, att, 0.0).astype(BF16)
            o = _dot(att, v_h[:r1, :]) + _dot_nt(q_inter_bf[ib][:, lo:hi], st_bf)
            gate = _silu(g_ref[r0:r1, lo:hi])
            y_ref[r0:r1, lo:hi] = (_rmsnorm_rows(o, normw_ref[:, lo:hi]) * gate).astype(y_ref.dtype)
        state_ref[lo:hi, :] = (st_t * state_decay[:, lo:hi]
                               + _dot(v[:, lo:hi].T.astype(BF16), k_end_bf[:, lo:hi]))


def _hgrn2_call(q, f, i, g, lb, normw, bsz, lc, sb):
    t, kd = q.shape
    nc = t // bsz // lc
    tok = lambda b, c: (b * nc + c, 0)
    const = lambda b, c: (0, 0)
    return pl.pallas_call(
        functools.partial(_hgrn2_kernel, lc, sb),
        grid=(bsz, nc),
        in_specs=[
            pl.BlockSpec((lc, kd), tok),
            pl.BlockSpec((lc, kd), tok),
            pl.BlockSpec((lc, kd), tok),
            pl.BlockSpec((lc, kd), tok),
            pl.BlockSpec(lb.shape, const),
            pl.BlockSpec(normw.shape, const),
        ],
        out_specs=pl.BlockSpec((lc, kd), tok),
        out_shape=jax.ShapeDtypeStruct((t, kd), BF16),
        scratch_shapes=[pltpu.VMEM((HG_HEADS * HG_DIM, HG_DIM), F32)],
        compiler_params=_params(("parallel", "arbitrary")),
        name="hgrn2",
    )(q, f, i, g, lb, normw)


def _outproj_kernel(x_ref, yssd_ref, yhg_ref, wout_ref, lnw_ref, wq_hi_ref, wq_lo_ref,
                    keys_hi_ref, keys_lo_ref, h_ref, hnt_ref, st_ref):
    inner = yssd_ref.shape[1]
    h = x_ref[...] + _dot(yssd_ref[...], wout_ref[:inner, :]) + _dot(yhg_ref[...], wout_ref[inner:, :])
    h_ref[...] = h
    hn_t = _rmsnorm_rows(h, lnw_ref[...]).T
    x_hi, x_lo = _split2(hn_t)
    hnt_ref[...] = x_hi
    wq_hi = wq_hi_ref[...]
    q_t = _dot(wq_hi, x_hi) + _dot(wq_hi, x_lo) + _dot(wq_lo_ref[...], x_hi)
    n_hp = keys_hi_ref.shape[0]
    half = keys_hi_ref.shape[2]
    for hp in range(n_hp):
        q_hi, q_lo = _split2(q_t[hp * half:(hp + 1) * half, :])
        k_hi = keys_hi_ref[hp]
        st_ref[hp] = _dot(k_hi, q_hi) + _dot(k_hi, q_lo) + _dot(keys_lo_ref[hp], q_hi)


def _outproj_call(x2, y_ssd, y_hg, w_out, lnw, wq_hi, wq_lo, keys_hi, keys_lo, tm):
    t, d = x2.shape
    n_hp, n_keys, half = keys_hi.shape
    const2 = lambda i: (0, 0)
    const3 = lambda i: (0, 0, 0)
    one = pl.Buffered(1)
    return pl.pallas_call(
        _outproj_kernel,
        grid=(t // tm,),
        in_specs=[
            pl.BlockSpec((tm, d), lambda i: (i, 0)),
            pl.BlockSpec((tm, y_ssd.shape[1]), lambda i: (i, 0)),
            pl.BlockSpec((tm, y_hg.shape[1]), lambda i: (i, 0)),
            pl.BlockSpec(w_out.shape, const2, pipeline_mode=one),
            pl.BlockSpec(lnw.shape, const2),
            pl.BlockSpec(wq_hi.shape, const2, pipeline_mode=one),
            pl.BlockSpec(wq_lo.shape, const2, pipeline_mode=one),
            pl.BlockSpec(keys_hi.shape, const3, pipeline_mode=one),
            pl.BlockSpec(keys_lo.shape, const3, pipeline_mode=one),
        ],
        out_specs=[
            pl.BlockSpec((tm, d), lambda i: (i, 0)),
            pl.BlockSpec((d, tm), lambda i: (0, i)),
            pl.BlockSpec((n_hp, n_keys, tm), lambda i: (0, 0, i)),
        ],
        out_shape=[
            jax.ShapeDtypeStruct((t, d), F32),
            jax.ShapeDtypeStruct((d, t), BF16),
            jax.ShapeDtypeStruct((n_hp, n_keys, t), F32),
        ],
        compiler_params=_params(("parallel",)),
        name="outproj",
    )(x2, y_ssd, y_hg, w_out, lnw, wq_hi, wq_lo, keys_hi, keys_lo)


def _extract_top(v, n_take, row_iota, out_ref):
    n_rows = v.shape[0]
    for r in range(n_take):
        m = jnp.max(v, axis=0, keepdims=True)
        out_ref[r:r + 1, :] = m
        if r + 1 < n_take:
            first = jnp.min(jnp.where(v == m, row_iota, n_rows), axis=0, keepdims=True)
            v = jnp.where(row_iota == first, -jnp.inf, v)


def _topk_kernel(st_ref, tau_ref, a1_ref, e2_ref, t1_ref, t2_ref, tc_ref):
    n_keys = st_ref.shape[1]
    tk = st_ref.shape[2]
    k = PEER_TOPK
    key_iota = lax.broadcasted_iota(jnp.int32, (n_keys, LANES), 0)
    k_iota = lax.broadcasted_iota(jnp.int32, (k, LANES), 0)
    cand_rows = k + (k // 2 - 1) * SUBLANES + SUBLANES
    cand_iota = lax.broadcasted_iota(jnp.int32, (cand_rows, LANES), 0)

    def head_strip(idx, carry):
        s = idx // PEER_HEADS
        h = idx % PEER_HEADS
        ls = pl.ds(pl.multiple_of(s * LANES, LANES), LANES)
        s1 = st_ref[2 * h, :, ls]
        s2 = st_ref[2 * h + 1, :, ls]
        _extract_top(s1, k, key_iota, t1_ref)
        _extract_top(s2, k, key_iota, t2_ref)
        t1 = t1_ref[...]
        t2 = t2_ref[...]
        tiles = [t1[0:1, :] + t2]
        for a in range(1, k // 2):
            nb = k // (a + 1)
            tiles.append(jnp.where(k_iota[:SUBLANES] < nb, t1[a:a + 1, :] + t2[:SUBLANES, :],
                                   -jnp.inf))
        tiles.append(t1[k // 2:, :] + t2[0:1, :])
        cand = jnp.concatenate(tiles, axis=0)
        _extract_top(cand, k, cand_iota, tc_ref)
        top = tc_ref[...]
        cmax = top[0:1, :]
        z = jnp.sum(jnp.exp(top - cmax), axis=0, keepdims=True)
        tau_ref[h, :, ls] = top[k - 1:k, :]
        a1_ref[h, :, ls] = jnp.exp(s1 - t1[0:1, :]) / z
        e2_ref[h, :, ls] = jnp.exp(s2 - t2[0:1, :])
        return carry

    lax.fori_loop(0, (tk // LANES) * PEER_HEADS, head_strip, 0)


def _topk_call(st, tk):
    n_hp, n_keys, t = st.shape
    nh = n_hp // 2
    return pl.pallas_call(
        _topk_kernel,
        grid=(t // tk,),
        in_specs=[pl.BlockSpec((n_hp, n_keys, tk), lambda i: (0, 0, i))],
        out_specs=[
            pl.BlockSpec((nh, 1, tk), lambda i: (0, 0, i)),
            pl.BlockSpec((nh, n_keys, tk), lambda i: (0, 0, i)),
            pl.BlockSpec((nh, n_keys, tk), lambda i: (0, 0, i)),
        ],
        out_shape=[
            jax.ShapeDtypeStruct((nh, 1, t), F32),
            jax.ShapeDtypeStruct((nh, n_keys, t), F32),
            jax.ShapeDtypeStruct((nh, n_keys, t), F32),
        ],
        scratch_shapes=[
            pltpu.VMEM((PEER_TOPK, LANES), F32),
            pltpu.VMEM((PEER_TOPK, LANES), F32),
            pltpu.VMEM((PEER_TOPK, LANES), F32),
        ],
        compiler_params=_params(("parallel",)),
        name="peer_topk",
    )(st)


def _peer_kernel(ib, hnt_ref, st_ref, s1r_ref, tau_ref, a1r_ref, e2_ref, u_ref, vt_ref, h_ref,
                 lnf_ref, out_ref, acc_ref, s_ref, hid_ref):
    step = pl.program_id(1)
    n_keys = st_ref.shape[1]
    tt = hnt_ref.shape[1]
    sqrt_half = np.float32(np.sqrt(0.5))

    @pl.when(step == 0)
    def _():
        acc_ref[...] = jnp.zeros(acc_ref.shape, F32)

    s_ref[...] = _dot(u_ref[...], hnt_ref[...])

    def strip(s, carry):
        ls = pl.ds(pl.multiple_of(s * LANES, LANES), LANES)
        for ii in range(ib):
            gate = jnp.zeros((n_keys, LANES), F32)
            for h in range(PEER_HEADS):
                s1_row = s1r_ref[2 * h, ii:ii + 1, ls]
                a1_row = a1r_ref[h, ii:ii + 1, ls]
                keep = (s1_row + st_ref[2 * h + 1, :, ls]) >= tau_ref[h, :, ls]
                gate = gate + jnp.where(keep, a1_row * e2_ref[h, :, ls], 0.0)
            pre = s_ref[ii * n_keys:(ii + 1) * n_keys, ls]
            act = 0.5 * pre * (1.0 + lax.erf(pre * sqrt_half))
            hid_ref[ii * n_keys:(ii + 1) * n_keys, ls] = (act * gate).astype(BF16)
        return carry

    lax.fori_loop(0, tt // LANES, strip, 0)
    acc_ref[...] += _dot(vt_ref[...], hid_ref[...])

    @pl.when(step == pl.num_programs(1) - 1)
    def _():
        total = h_ref[...] + acc_ref[...].T
        out_ref[...] = _rmsnorm_rows(total, lnf_ref[...])


def _peer_call(hnt, st, tau, a1, e2, u_bf, vt_bf, h, lnf, tt, ib):
    d, t = hnt.shape
    n_hp, n_keys, _ = st.shape
    nh = n_hp // 2
    n_exp = u_bf.shape[0]
    eb = ib * n_keys
    return pl.pallas_call(
        functools.partial(_peer_kernel, ib),
        grid=(t // tt, n_exp // eb),
        in_specs=[
            pl.BlockSpec((d, tt), lambda i, j: (0, i)),
            pl.BlockSpec((n_hp, n_keys, tt), lambda i, j: (0, 0, i)),
            pl.BlockSpec((n_hp, ib, tt), lambda i, j: (0, j, i)),
            pl.BlockSpec((nh, 1, tt), lambda i, j: (0, 0, i)),
            pl.BlockSpec((nh, ib, tt), lambda i, j: (0, j, i)),
            pl.BlockSpec((nh, n_keys, tt), lambda i, j: (0, 0, i)),
            pl.BlockSpec((eb, d), lambda i, j: (j, 0)),
            pl.BlockSpec((d, eb), lambda i, j: (0, j)),
            pl.BlockSpec((tt, d), lambda i, j: (i, 0)),
            pl.BlockSpec((1, d), lambda i, j: (0, 0)),
        ],
        out_specs=pl.BlockSpec((tt, d), lambda i, j: (i, 0)),
        out_shape=jax.ShapeDtypeStruct((t, d), F32),
        scratch_shapes=[
            pltpu.VMEM((d, tt), F32),
            pltpu.VMEM((eb, tt), F32),
            pltpu.VMEM((eb, tt), BF16),
        ],
        compiler_params=_params(("parallel", "arbitrary")),
        name="peer_experts",
    )(hnt, st, st, tau, a1, e2, u_bf, vt_bf, h, lnf)


def _layer(h2, bsz, lb, ln_mix_w, w_in, conv_w, conv_b, dt_bias, a_log, d_skip, ssd_norm_w,
           hg_norm_w, w_out, ln_ffn_w, w_query, sub_keys, u_table, v_table, ln_out_w,
           tm, lc_ssd, lc_hg, sb_hg, tk, tt, ib):
    t, d = h2.shape
    inner = SSD_HEADS * SSD_HEAD_DIM
    conv_ch = inner + 2 * SSD_GROUPS * SSD_STATE
    hgk = HG_HEADS * HG_DIM

    sizes = (inner, conv_ch, SSD_HEADS, hgk, hgk, hgk, hgk)
    offs = np.concatenate([[0], np.cumsum(sizes)])
    wz, wxbc, wdt, wq_, wf, wi, wg = [w_in[:, offs[n]:offs[n + 1]] for n in range(7)]
    wdt_pad = jnp.pad(wdt, ((0, 0), (0, LANES - SSD_HEADS)))
    w_pad = jnp.concatenate([wz, wxbc, wq_, wf, wi, wg, wdt_pad], axis=1).astype(BF16)
    widths = (inner, conv_ch, hgk, hgk, hgk, hgk, LANES)
    starts = np.concatenate([[0], np.cumsum(widths)])[:-1]
    col_slices = tuple((int(s), int(w)) for s, w in zip(starts, widths))

    z, xbc, q_raw, f_raw, i_raw, g_raw, dt_raw = _inproj_call(
        h2, ln_mix_w.reshape(1, d), w_pad, col_slices, tm)

    pad_h = (0, LANES - SSD_HEADS)
    dtb = jnp.pad(dt_bias.astype(F32), pad_h).reshape(1, LANES)
    ahead = jnp.pad(-jnp.exp(a_log.astype(F32)), pad_h).reshape(1, LANES)
    dskip_x = jnp.repeat(d_skip.astype(F32), SSD_HEAD_DIM).reshape(1, inner)
    expand = (jnp.arange(LANES)[:, None] == (jnp.arange(inner)[None, :] // SSD_HEAD_DIM)).astype(BF16)
    y_ssd = _ssd_call(z, xbc, dt_raw, conv_w.astype(F32), conv_b.reshape(1, conv_ch).astype(F32),
                      dtb, ahead, dskip_x, ssd_norm_w.reshape(1, inner), expand, bsz, lc_ssd)

    y_hg = _hgrn2_call(q_raw, f_raw, i_raw, g_raw, lb.reshape(1, hgk), hg_norm_w.reshape(1, hgk),
                       bsz, lc_hg, sb_hg)

    wq_hi, wq_lo = _split2(w_query.T.astype(F32))
    n_hp = PEER_HEADS * 2
    half = sub_keys.shape[-1]
    keys_hi, keys_lo = _split2(sub_keys.reshape(n_hp, PEER_KEYS, half).astype(F32))
    h_res, hn_t, st = _outproj_call(h2, y_ssd, y_hg, w_out.astype(BF16), ln_ffn_w.reshape(1, d),
                                    wq_hi, wq_lo, keys_hi, keys_lo, tm)

    tau, a1, e2 = _topk_call(st, tk)

    return _peer_call(hn_t, st, tau, a1, e2, u_table.astype(BF16), v_table.T.astype(BF16),
                      h_res, ln_out_w.reshape(1, d), tt, ib)


def kernel(x, ln_mix_w, w_in, conv_w, conv_b, dt_bias, a_log, d_skip, ssd_norm_w, lb_logits,
           hg_norm_w, w_out, ln_ffn_w, w_query, sub_keys, u_table, v_table, ln_final_w):
    bsz, seqlen, d = x.shape
    depth = w_in.shape[0]
    assert depth == 1, "the final RMSNorm is fused into the (single) layer's expert kernel"
    lb_all = jnp.cumsum(jax.nn.softmax(lb_logits.astype(F32), axis=0), axis=0)
    t = bsz * seqlen
    tm = min(256, t)
    lc = min(128, seqlen)
    out = _layer(x.reshape(t, d), bsz, lb_all[0], ln_mix_w[0], w_in[0], conv_w[0], conv_b[0],
                 dt_bias[0], a_log[0], d_skip[0], ssd_norm_w[0], hg_norm_w[0], w_out[0],
                 ln_ffn_w[0], w_query[0], sub_keys[0], u_table[0], v_table[0], ln_final_w,
                 tm=tm, lc_ssd=lc, lc_hg=min(64, seqlen), sb_hg=32, tk=min(512, t),
                 tt=min(512, t), ib=SUBLANES)
    return out.reshape(bsz, seqlen, d)
```

```python
import functools

import numpy as np
import jax
import jax.numpy as jnp
from jax import lax
from jax.experimental import pallas as pl
from jax.experimental.pallas import tpu as pltpu

F32 = jnp.float32
BF16 = jnp.bfloat16
EPS = 1e-6

SSD_HEADS = 16
SSD_HEAD_DIM = 64
SSD_GROUPS = 4
SSD_STATE = 128
SSD_CONV = 4
HG_HEADS = 8
HG_DIM = 128
PEER_HEADS = 8
PEER_KEYS = 128
PEER_TOPK = 16

LANES = 128
SUBLANES = 8
VMEM_LIMIT = 56 * 1024 * 1024

CONV_PAD = SUBLANES


def _params(sem):
    return pltpu.CompilerParams(dimension_semantics=sem, vmem_limit_bytes=VMEM_LIMIT)


def _split3(x):
    hi = x.astype(BF16)
    r1 = x - hi.astype(F32)
    mid = r1.astype(BF16)
    lo = (r1 - mid.astype(F32)).astype(BF16)
    return hi, mid, lo


def _split2(x):
    hi = x.astype(BF16)
    lo = (x - hi.astype(F32)).astype(BF16)
    return hi, lo


def _dot(a, b):
    return jnp.dot(a, b, preferred_element_type=F32)


def _dot_nt(a, b):
    return lax.dot_general(a, b, (((1,), (1,)), ((), ())), preferred_element_type=F32)


def _dot_exact_rhs(m01, x):
    hi, mid, lo = _split3(x)
    return _dot(m01, hi) + _dot(m01, mid) + _dot(m01, lo)


def _dot_exact_lhs(x, m01):
    hi, mid, lo = _split3(x)
    return _dot(hi, m01) + _dot(mid, m01) + _dot(lo, m01)


def _silu(x):
    return x * jax.nn.sigmoid(x)


def _softplus(x):
    return jnp.maximum(x, 0.0) + jnp.log1p(jnp.exp(-jnp.abs(x)))


def _rmsnorm_rows(x, w):
    ms = jnp.mean(x * x, axis=-1, keepdims=True)
    return x * lax.rsqrt(ms + EPS) * w


def _inproj_kernel(col_slices, x_ref, lnw_ref, w_ref, *out_refs):
    hn = _rmsnorm_rows(x_ref[...], lnw_ref[...]).astype(BF16)
    for ref, (off, width) in zip(out_refs, col_slices):
        ref[...] = _dot(hn, w_ref[:, off:off + width]).astype(ref.dtype)


def _inproj_call(x2, lnw, w_pad, col_slices, tm):
    t, d = x2.shape
    out_shape = [jax.ShapeDtypeStruct((t, width), F32) for _, width in col_slices]
    out_specs = [pl.BlockSpec((tm, width), lambda i: (i, 0)) for _, width in col_slices]
    return pl.pallas_call(
        functools.partial(_inproj_kernel, col_slices),
        grid=(t // tm,),
        in_specs=[
            pl.BlockSpec((tm, d), lambda i: (i, 0)),
            pl.BlockSpec((1, d), lambda i: (0, 0)),
            pl.BlockSpec(w_pad.shape, lambda i: (0, 0), pipeline_mode=pl.Buffered(1)),
        ],
        out_specs=out_specs,
        out_shape=out_shape,
        compiler_params=_params(("parallel",)),
        name="inproj",
    )(x2, lnw, w_pad)


def _ssd_kernel(lc, z_ref, xbc_ref, dt_ref, convw_ref, convb_ref, dtb_ref, ahead_ref, dskip_ref,
                normw_ref, expand_ref, y_ref, xpad_ref, state_ref, ybuf_ref):
    inner = SSD_HEADS * SSD_HEAD_DIM
    gn = SSD_GROUPS * SSD_STATE
    pair_w = 2 * SSD_HEAD_DIM

    @pl.when(pl.program_id(1) == 0)
    def _():
        xpad_ref[0:CONV_PAD, :] = jnp.zeros((CONV_PAD, xpad_ref.shape[1]), F32)
        state_ref[...] = jnp.zeros(state_ref.shape, F32)

    xpad_ref[CONV_PAD:CONV_PAD + lc, :] = xbc_ref[...]
    conv = convb_ref[...]
    for k in range(SSD_CONV):
        start = CONV_PAD - (SSD_CONV - 1) + k
        conv = conv + convw_ref[k:k + 1, :] * xpad_ref[start:start + lc, :]
    xpad_ref[0:CONV_PAD, :] = xpad_ref[lc:lc + CONV_PAD, :]
    xbc = _silu(conv)
    xs = xbc[:, :inner]

    dt = _softplus(dt_ref[...] + dtb_ref[...])
    a = dt * ahead_ref[...]

    row = lax.broadcasted_iota(jnp.int32, (lc, lc), 0)
    col = lax.broadcasted_iota(jnp.int32, (lc, lc), 1)
    tril = row >= col
    tri_bf = tril.astype(BF16)
    acum = _dot_exact_rhs(tri_bf, a)
    acum_t = acum.T
    a_last = acum[lc - 1:lc, :]

    expand = expand_ref[...]
    dt_x = _dot_exact_lhs(dt, expand)
    from_start_x = _dot_exact_lhs(jnp.exp(acum), expand)
    to_end_x = _dot_exact_lhs(jnp.exp(a_last - acum), expand)
    chunk_decay = jnp.exp(a_last)
    xdt = xs * dt_x
    xw = xdt * to_end_x

    lane = lax.broadcasted_iota(jnp.int32, (lc, pair_w), 1)
    first_half = lane < SSD_HEAD_DIM

    heads_per_group = SSD_HEADS // SSD_GROUPS
    for g in range(SSD_GROUPS):
        bm = xbc[:, inner + g * SSD_STATE: inner + (g + 1) * SSD_STATE].astype(BF16)
        cm = xbc[:, inner + gn + g * SSD_STATE: inner + gn + (g + 1) * SSD_STATE].astype(BF16)
        cb = _dot_nt(cm, bm)
        for pr in range(heads_per_group // 2):
            pidx = g * (heads_per_group // 2) + pr
            lo, hi = pidx * pair_w, (pidx + 1) * pair_w
            xdt_p = xdt[:, lo:hi]
            y_pair = None
            for sub in range(2):
                h = 2 * pidx + sub
                diff = acum[:, h:h + 1] - acum_t[h:h + 1, :]
                decay = jnp.where(tril, jnp.exp(jnp.where(tril, diff, 0.0)), 0.0)
                m_h = (cb * decay).astype(BF16)
                sel = first_half if sub == 0 else jnp.logical_not(first_half)
                contrib = _dot(m_h, jnp.where(sel, xdt_p, 0.0).astype(BF16))
                y_pair = contrib if y_pair is None else y_pair + contrib
            st = state_ref[lo:hi, :]
            y_off = _dot_nt(cm, st.astype(BF16)) * from_start_x[:, lo:hi]
            ybuf_ref[:, lo:hi] = y_pair + y_off + xs[:, lo:hi] * dskip_ref[:, lo:hi]
            r_idx = lax.broadcasted_iota(jnp.int32, (pair_w, SSD_STATE), 0)
            dec = jnp.where(r_idx < SSD_HEAD_DIM, chunk_decay[:, 2 * pidx:2 * pidx + 1],
                            chunk_decay[:, 2 * pidx + 1:2 * pidx + 2])
            state_ref[lo:hi, :] = st * dec + _dot(xw[:, lo:hi].T.astype(BF16), bm)

    y = ybuf_ref[...] * _silu(z_ref[...])
    y_ref[...] = _rmsnorm_rows(y, normw_ref[...]).astype(y_ref.dtype)


def _ssd_call(z, xbc, dtp, convw, convb, dtb, ahead, dskip_x, normw, expand, bsz, lc):
    t, inner = z.shape
    cw = xbc.shape[1]
    nc = t // bsz // lc
    tok = lambda b, c: (b * nc + c, 0)
    const = lambda b, c: (0, 0)
    return pl.pallas_call(
        functools.partial(_ssd_kernel, lc),
        grid=(bsz, nc),
        in_specs=[
            pl.BlockSpec((lc, inner), tok),
            pl.BlockSpec((lc, cw), tok),
            pl.BlockSpec((lc, LANES), tok),
            pl.BlockSpec(convw.shape, const),
            pl.BlockSpec(convb.shape, const),
            pl.BlockSpec(dtb.shape, const),
            pl.BlockSpec(ahead.shape, const),
            pl.BlockSpec(dskip_x.shape, const),
            pl.BlockSpec(normw.shape, const),
            pl.BlockSpec(expand.shape, const),
        ],
        out_specs=pl.BlockSpec((lc, inner), tok),
        out_shape=jax.ShapeDtypeStruct((t, inner), BF16),
        scratch_shapes=[
            pltpu.VMEM((lc + CONV_PAD, cw), F32),
            pltpu.VMEM((inner, SSD_STATE), F32),
            pltpu.VMEM((lc, inner), F32),
        ],
        compiler_params=_params(("parallel", "arbitrary")),
        name="ssd",
    )(z, xbc, dtp, convw, convb, dtb, ahead, dskip_x, normw, expand)


def _hgrn2_kernel(lc, sb, q_ref, f_ref, i_ref, g_ref, lb_ref, normw_ref, y_ref, state_ref):
    nsb = lc // sb

    @pl.when(pl.program_id(1) == 0)
    def _():
        state_ref[...] = jnp.zeros(state_ref.shape, F32)

    lb = lb_ref[...]
    f = lb + (1.0 - lb) * jax.nn.sigmoid(f_ref[...])
    logf = jnp.log(f)
    kk = 1.0 - f
    q = _silu(q_ref[...])
    v = i_ref[...]
    v_bf = v.astype(BF16)

    row = lax.broadcasted_iota(jnp.int32, (lc, lc), 0)
    col = lax.broadcasted_iota(jnp.int32, (lc, lc), 1)
    tri_bf = (row >= col).astype(BF16)
    gcum = _dot_exact_rhs(tri_bf, logf)
    g_end = gcum[lc - 1:lc, :]
    k_end_bf = (kk * jnp.exp(g_end - gcum)).astype(BF16)
    state_decay = jnp.exp(g_end)

    qs_bf, q_inter_bf, k_ref_bf = [], [], []
    for ib in range(nsb):
        r0, r1 = ib * sb, (ib + 1) * sb
        if ib == 0:
            gref = jnp.zeros_like(g_end)
        else:
            gref = gcum[r0 - 1:r0, :]
        e_q = jnp.exp(gcum[r0:r1, :] - gref)
        qs = q[r0:r1, :] * e_q
        qs_bf.append(qs.astype(BF16))
        q_inter_bf.append((qs * jnp.exp(gref)).astype(BF16))
        k_ref_bf.append((kk[:r1, :] * jnp.exp(gref - gcum[:r1, :])).astype(BF16))

    for h in range(HG_HEADS):
        lo, hi = h * HG_DIM, (h + 1) * HG_DIM
        st_t = state_ref[lo:hi, :]
        st_bf = st_t.astype(BF16)
        v_h = v_bf[:, lo:hi]
        for ib in range(nsb):
            r0, r1 = ib * sb, (ib + 1) * sb
            att = _dot_nt(qs_bf[ib][:, lo:hi], k_ref_bf[ib][:, lo:hi])
            rr = lax.broadcasted_iota(jnp.int32, (sb, r1), 0) + r0
            cc = lax.broadcasted_iota(jnp.int32, (sb, r1), 1)
            att = jnp.where(rr >= cc, att, 0.0).astype(BF16)
            o = _dot(att, v_h[:r1, :]) + _dot_nt(q_inter_bf[ib][:, lo:hi], st_bf)
            gate = _silu(g_ref[r0:r1, lo:hi])
            y_ref[r0:r1, lo:hi] = (_rmsnorm_rows(o, normw_ref[:, lo:hi]) * gate).astype(y_ref.dtype)
        state_ref[lo:hi, :] = (st_t * state_decay[:, lo:hi]
                               + _dot(v[:, lo:hi].T.astype(BF16), k_end_bf[:, lo:hi]))


def _hgrn2_call(q, f, i, g, lb, normw, bsz, lc, sb):
    t, kd = q.shape
    nc = t // bsz // lc
    tok = lambda b, c: (b * nc + c, 0)
    const = lambda b, c: (0, 0)
    return pl.pallas_call(
        functools.partial(_hgrn2_kernel, lc, sb),
        grid=(bsz, nc),
        in_specs=[
            pl.BlockSpec((lc, kd), tok),
            pl.BlockSpec((lc, kd), tok),
            pl.BlockSpec((lc, kd), tok),
            pl.BlockSpec((lc, kd), tok),
            pl.BlockSpec(lb.shape, const),
            pl.BlockSpec(normw.shape, const),
        ],
        out_specs=pl.BlockSpec((lc, kd), tok),
        out_shape=jax.ShapeDtypeStruct((t, kd), BF16),
        scratch_shapes=[pltpu.VMEM((HG_HEADS * HG_DIM, HG_DIM), F32)],
        compiler_params=_params(("parallel", "arbitrary")),
        name="hgrn2",
    )(q, f, i, g, lb, normw)


def _outproj_kernel(x_ref, yssd_ref, yhg_ref, wout_ref, lnw_ref, wq_hi_ref, wq_lo_ref,
                    keys_hi_ref, keys_lo_ref, h_ref, hnt_ref, st_ref):
    inner = yssd_ref.shape[1]
    h = x_ref[...] + _dot(yssd_ref[...], wout_ref[:inner, :]) + _dot(yhg_ref[...], wout_ref[inner:, :])
    h_ref[...] = h
    hn_t = _rmsnorm_rows(h, lnw_ref[...]).T
    x_hi, x_lo = _split2(hn_t)
    hnt_ref[...] = x_hi
    wq_hi = wq_hi_ref[...]
    q_t = _dot(wq_hi, x_hi) + _dot(wq_hi, x_lo) + _dot(wq_lo_ref[...], x_hi)
    n_hp = keys_hi_ref.shape[0]
    half = keys_hi_ref.shape[2]
    for hp in range(n_hp):
        q_hi, q_lo = _split2(q_t[hp * half:(hp + 1) * half, :])
        k_hi = keys_hi_ref[hp]
        st_ref[hp] = _dot(k_hi, q_hi) + _dot(k_hi, q_lo) + _dot(keys_lo_ref[hp], q_hi)


def _outproj_call(x2, y_ssd, y_hg, w_out, lnw, wq_hi, wq_lo, keys_hi, keys_lo, tm):
    t, d = x2.shape
    n_hp, n_keys, half = keys_hi.shape
    const2 = lambda i: (0, 0)
    const3 = lambda i: (0, 0, 0)
    one = pl.Buffered(1)
    return pl.pallas_call(
        _outproj_kernel,
        grid=(t // tm,),
        in_specs=[
            pl.BlockSpec((tm, d), lambda i: (i, 0)),
            pl.BlockSpec((tm, y_ssd.shape[1]), lambda i: (i, 0)),
            pl.BlockSpec((tm, y_hg.shape[1]), lambda i: (i, 0)),
            pl.BlockSpec(w_out.shape, const2, pipeline_mode=one),
            pl.BlockSpec(lnw.shape, const2),
            pl.BlockSpec(wq_hi.shape, const2, pipeline_mode=one),
            pl.BlockSpec(wq_lo.shape, const2, pipeline_mode=one),
            pl.BlockSpec(keys_hi.shape, const3, pipeline_mode=one),
            pl.BlockSpec(keys_lo.shape, const3, pipeline_mode=one),
        ],
        out_specs=[
            pl.BlockSpec((tm, d), lambda i: (i, 0)),
            pl.BlockSpec((d, tm), lambda i: (0, i)),
            pl.BlockSpec((n_hp, n_keys, tm), lambda i: (0, 0, i)),
        ],
        out_shape=[
            jax.ShapeDtypeStruct((t, d), F32),
            jax.ShapeDtypeStruct((d, t), BF16),
            jax.ShapeDtypeStruct((n_hp, n_keys, t), F32),
        ],
        compiler_params=_params(("parallel",)),
        name="outproj",
    )(x2, y_ssd, y_hg, w_out, lnw, wq_hi, wq_lo, keys_hi, keys_lo)


NO_RANK = 255.0


def _extract_top(v, n_take, row_iota, out_ref, want_rank):
    n_rows = v.shape[0]
    rank = jnp.full(v.shape, NO_RANK, F32) if want_rank else None
    for r in range(n_take):
        m = jnp.max(v, axis=0, keepdims=True)
        out_ref[r:r + 1, :] = m
        if want_rank or r + 1 < n_take:
            first = jnp.min(jnp.where(v == m, row_iota, n_rows), axis=0, keepdims=True)
            hit = row_iota == first
            if want_rank:
                rank = jnp.where(hit, np.float32(r), rank)
            if r + 1 < n_take:
                v = jnp.where(hit, -jnp.inf, v)
    return rank


def _topk_kernel(st_ref, r2_ref, e2_ref, n1_ref, a1_ref, t1_ref, t2_ref, tc_ref):
    n_keys = st_ref.shape[1]
    tk = st_ref.shape[2]
    k = PEER_TOPK
    key_iota = lax.broadcasted_iota(jnp.int32, (n_keys, LANES), 0)
    k_iota = lax.broadcasted_iota(jnp.int32, (k, LANES), 0)
    cand_rows = k + (k // 2 - 1) * SUBLANES + SUBLANES
    cand_iota = lax.broadcasted_iota(jnp.int32, (cand_rows, LANES), 0)

    def head_strip(idx, carry):
        s = idx // PEER_HEADS
        h = idx % PEER_HEADS
        ls = pl.ds(pl.multiple_of(s * LANES, LANES), LANES)
        s1 = st_ref[2 * h, :, ls]
        s2 = st_ref[2 * h + 1, :, ls]
        rank1 = _extract_top(s1, k, key_iota, t1_ref, True)
        rank2 = _extract_top(s2, k, key_iota, t2_ref, True)
        t1 = t1_ref[...]
        t2 = t2_ref[...]
        tiles = [t1[0:1, :] + t2]
        for a in range(1, k // 2):
            nb = k // (a + 1)
            tiles.append(jnp.where(k_iota[:SUBLANES] < nb, t1[a:a + 1, :] + t2[:SUBLANES, :],
                                   -jnp.inf))
        tiles.append(t1[k // 2:, :] + t2[0:1, :])
        cand = jnp.concatenate(tiles, axis=0)
        _extract_top(cand, k, cand_iota, tc_ref, False)
        top = tc_ref[...]
        cmax = top[0:1, :]
        z = jnp.sum(jnp.exp(top - cmax), axis=0, keepdims=True)
        tau = top[k - 1:k, :]
        passed = jnp.where(cand >= tau, 1.0, 0.0)
        counts = [jnp.sum(passed[0:k, :], axis=0, keepdims=True)]
        for a in range(1, k // 2):
            r0 = k + (a - 1) * SUBLANES
            counts.append(jnp.sum(passed[r0:r0 + SUBLANES, :], axis=0, keepdims=True))
        r0 = k + (k // 2 - 1) * SUBLANES
        counts += [passed[r0 + a:r0 + a + 1, :] for a in range(k - k // 2)]
        n_row = jnp.zeros((n_keys, LANES), F32)
        for a in range(k):
            n_row = jnp.where(rank1 == np.float32(a), counts[a], n_row)
        r2_ref[h, :, ls] = rank2
        e2_ref[h, :, ls] = jnp.exp(s2 - t2[0:1, :])
        n1_ref[h, :, ls] = n_row
        a1_ref[h, :, ls] = jnp.exp(s1 - t1[0:1, :]) / z
        return carry

    lax.fori_loop(0, (tk // LANES) * PEER_HEADS, head_strip, 0)


def _topk_call(st, tk):
    n_hp, n_keys, t = st.shape
    nh = n_hp // 2
    blk = pl.BlockSpec((nh, n_keys, tk), lambda i: (0, 0, i))
    per_key = jax.ShapeDtypeStruct((nh, n_keys, t), F32)
    return pl.pallas_call(
        _topk_kernel,
        grid=(t // tk,),
        in_specs=[pl.BlockSpec((n_hp, n_keys, tk), lambda i: (0, 0, i))],
        out_specs=[blk, blk, blk, blk],
        out_shape=[per_key, per_key, per_key, per_key],
        scratch_shapes=[
            pltpu.VMEM((PEER_TOPK, LANES), F32),
            pltpu.VMEM((PEER_TOPK, LANES), F32),
            pltpu.VMEM((PEER_TOPK, LANES), F32),
        ],
        compiler_params=_params(("parallel",)),
        name="peer_topk",
    )(st)


BF16_ROWS = 2 * SUBLANES
MM_ROWS = 256


def _peer_kernel(ib, hnt_ref, r2_ref, e2_ref, n1_ref, a1_ref, u_ref, vt_ref, h_ref, lnf_ref,
                 out_ref, acc_ref, s_ref, hid_ref, r2b_ref, e2b_ref):
    step = pl.program_id(1)
    n_keys = r2_ref.shape[1]
    tt = hnt_ref.shape[1]
    sqrt_half = np.float32(np.sqrt(0.5))
    per_chunk = MM_ROWS // n_keys
    n_chunks = ib // per_chunk

    @pl.when(step == 0)
    def _():
        acc_ref[...] = jnp.zeros(acc_ref.shape, F32)
        for h in range(PEER_HEADS):
            r2b_ref[h] = r2_ref[h].astype(BF16)
            e2b_ref[h] = e2_ref[h].astype(BF16)

    for c in range(n_chunks):
        rows = slice(c * MM_ROWS, (c + 1) * MM_ROWS)
        s_ref[rows, :] = _dot(u_ref[rows, :], hnt_ref[...])

    def bcast_rows(ref, h, ii, ls):
        return jnp.broadcast_to(ref[h, ii:ii + 1, ls], (BF16_ROWS, LANES)).astype(BF16)

    zero = jnp.zeros((BF16_ROWS, LANES), BF16)
    for c in range(n_chunks):
        for ii in range(c * per_chunk, (c + 1) * per_chunk):
            for s in range(tt // LANES):
                ls = slice(s * LANES, (s + 1) * LANES)
                n_rows = [bcast_rows(n1_ref, h, ii, ls) for h in range(PEER_HEADS)]
                a_rows = [bcast_rows(a1_ref, h, ii, ls) for h in range(PEER_HEADS)]
                for jt in range(n_keys // BF16_ROWS):
                    kr = slice(jt * BF16_ROWS, (jt + 1) * BF16_ROWS)
                    gate = None
                    for h in range(PEER_HEADS):
                        term = jnp.where(r2b_ref[h, kr, ls] < n_rows[h], e2b_ref[h, kr, ls], zero)
                        term = term * a_rows[h]
                        gate = term if gate is None else gate + term
                    er = slice(ii * n_keys + jt * BF16_ROWS, ii * n_keys + (jt + 1) * BF16_ROWS)
                    pre = s_ref[er, ls]
                    act = 0.5 * pre * (1.0 + lax.erf(pre * sqrt_half))
                    hid_ref[er, ls] = act.astype(BF16) * gate

    upd = None
    for c in range(n_chunks):
        rows = slice(c * MM_ROWS, (c + 1) * MM_ROWS)
        part = _dot(vt_ref[:, rows], hid_ref[rows, :])
        upd = part if upd is None else upd + part
    acc_ref[...] += upd

    @pl.when(step == pl.num_programs(1) - 1)
    def _():
        total = h_ref[...] + acc_ref[...].T
        out_ref[...] = _rmsnorm_rows(total, lnf_ref[...])


def _peer_call(hnt, r2, e2, n1, a1, u_bf, vt_bf, h, lnf, tt, ib):
    d, t = hnt.shape
    nh, n_keys, _ = r2.shape
    n_exp = u_bf.shape[0]
    eb = ib * n_keys
    assert eb % MM_ROWS == 0 and MM_ROWS % n_keys == 0
    full = pl.BlockSpec((nh, n_keys, tt), lambda i, j: (0, 0, i))
    rows = pl.BlockSpec((nh, ib, tt), lambda i, j: (0, j, i))
    return pl.pallas_call(
        functools.partial(_peer_kernel, ib),
        grid=(t // tt, n_exp // eb),
        in_specs=[
            pl.BlockSpec((d, tt), lambda i, j: (0, i)),
            full, full, rows, rows,
            pl.BlockSpec((eb, d), lambda i, j: (j, 0)),
            pl.BlockSpec((d, eb), lambda i, j: (0, j)),
            pl.BlockSpec((tt, d), lambda i, j: (i, 0)),
            pl.BlockSpec((1, d), lambda i, j: (0, 0)),
        ],
        out_specs=pl.BlockSpec((tt, d), lambda i, j: (i, 0)),
        out_shape=jax.ShapeDtypeStruct((t, d), F32),
        scratch_shapes=[
            pltpu.VMEM((d, tt), F32),
            pltpu.VMEM((eb, tt), F32),
            pltpu.VMEM((eb, tt), BF16),
            pltpu.VMEM((nh, n_keys, tt), BF16),
            pltpu.VMEM((nh, n_keys, tt), BF16),
        ],
        compiler_params=_params(("parallel", "arbitrary")),
        name="peer_experts",
    )(hnt, r2, e2, n1, a1, u_bf, vt_bf, h, lnf)


def _layer(h2, bsz, lb, ln_mix_w, w_in, conv_w, conv_b, dt_bias, a_log, d_skip, ssd_norm_w,
           hg_norm_w, w_out, ln_ffn_w, w_query, sub_keys, u_table, v_table, ln_out_w,
           tm, lc_ssd, lc_hg, sb_hg, tk, tt, ib):
    t, d = h2.shape
    inner = SSD_HEADS * SSD_HEAD_DIM
    conv_ch = inner + 2 * SSD_GROUPS * SSD_STATE
    hgk = HG_HEADS * HG_DIM

    sizes = (inner, conv_ch, SSD_HEADS, hgk, hgk, hgk, hgk)
    offs = np.concatenate([[0], np.cumsum(sizes)])
    wz, wxbc, wdt, wq_, wf, wi, wg = [w_in[:, offs[n]:offs[n + 1]] for n in range(7)]
    wdt_pad = jnp.pad(wdt, ((0, 0), (0, LANES - SSD_HEADS)))
    w_pad = jnp.concatenate([wz, wxbc, wq_, wf, wi, wg, wdt_pad], axis=1).astype(BF16)
    widths = (inner, conv_ch, hgk, hgk, hgk, hgk, LANES)
    starts = np.concatenate([[0], np.cumsum(widths)])[:-1]
    col_slices = tuple((int(s), int(w)) for s, w in zip(starts, widths))

    z, xbc, q_raw, f_raw, i_raw, g_raw, dt_raw = _inproj_call(
        h2, ln_mix_w.reshape(1, d), w_pad, col_slices, tm)

    pad_h = (0, LANES - SSD_HEADS)
    dtb = jnp.pad(dt_bias.astype(F32), pad_h).reshape(1, LANES)
    ahead = jnp.pad(-jnp.exp(a_log.astype(F32)), pad_h).reshape(1, LANES)
    dskip_x = jnp.repeat(d_skip.astype(F32), SSD_HEAD_DIM).reshape(1, inner)
    expand = (jnp.arange(LANES)[:, None] == (jnp.arange(inner)[None, :] // SSD_HEAD_DIM)).astype(BF16)
    y_ssd = _ssd_call(z, xbc, dt_raw, conv_w.astype(F32), conv_b.reshape(1, conv_ch).astype(F32),
                      dtb, ahead, dskip_x, ssd_norm_w.reshape(1, inner), expand, bsz, lc_ssd)

    y_hg = _hgrn2_call(q_raw, f_raw, i_raw, g_raw, lb.reshape(1, hgk), hg_norm_w.reshape(1, hgk),
                       bsz, lc_hg, sb_hg)

    wq_hi, wq_lo = _split2(w_query.T.astype(F32))
    n_hp = PEER_HEADS * 2
    half = sub_keys.shape[-1]
    keys_hi, keys_lo = _split2(sub_keys.reshape(n_hp, PEER_KEYS, half).astype(F32))
    h_res, hn_t, st = _outproj_call(h2, y_ssd, y_hg, w_out.astype(BF16), ln_ffn_w.reshape(1, d),
                                    wq_hi, wq_lo, keys_hi, keys_lo, tm)

    r2, e2, n1, a1 = _topk_call(st, tk)

    return _peer_call(hn_t, r2, e2, n1, a1, u_table.astype(BF16), v_table.T.astype(BF16),
                      h_res, ln_out_w.reshape(1, d), tt, ib)


def kernel(x, ln_mix_w, w_in, conv_w, conv_b, dt_bias, a_log, d_skip, ssd_norm_w, lb_logits,
           hg_norm_w, w_out, ln_ffn_w, w_query, sub_keys, u_table, v_table, ln_final_w):
    bsz, seqlen, d = x.shape
    depth = w_in.shape[0]
    assert depth == 1, "the final RMSNorm is fused into the (single) layer's expert kernel"
    lb_all = jnp.cumsum(jax.nn.softmax(lb_logits.astype(F32), axis=0), axis=0)
    t = bsz * seqlen
    tm = min(256, t)
    lc = min(128, seqlen)
    out = _layer(x.reshape(t, d), bsz, lb_all[0], ln_mix_w[0], w_in[0], conv_w[0], conv_b[0],
                 dt_bias[0], a_log[0], d_skip[0], ssd_norm_w[0], hg_norm_w[0], w_out[0],
                 ln_ffn_w[0], w_query[0], sub_keys[0], u_table[0], v_table[0], ln_final_w,
                 tm=tm, lc_ssd=lc, lc_hg=min(64, seqlen), sb_hg=32, tk=min(512, t),
                 tt=min(512, t), ib=SUBLANES)
    return out.reshape(bsz, seqlen, d)
```

```python
import functools

import numpy as np
import jax
import jax.numpy as jnp
from jax import lax
from jax.experimental import pallas as pl
from jax.experimental.pallas import tpu as pltpu

F32 = jnp.float32
BF16 = jnp.bfloat16
EPS = 1e-6

SSD_HEADS = 16
SSD_HEAD_DIM = 64
SSD_GROUPS = 4
SSD_STATE = 128
SSD_CONV = 4
HG_HEADS = 8
HG_DIM = 128
PEER_HEADS = 8
PEER_KEYS = 128
PEER_TOPK = 16

LANES = 128
SUBLANES = 8
VMEM_LIMIT = 56 * 1024 * 1024

CONV_PAD = SUBLANES


def _params(sem):
    return pltpu.CompilerParams(dimension_semantics=sem, vmem_limit_bytes=VMEM_LIMIT)


def _split3(x):
    hi = x.astype(BF16)
    r1 = x - hi.astype(F32)
    mid = r1.astype(BF16)
    lo = (r1 - mid.astype(F32)).astype(BF16)
    return hi, mid, lo


def _split2(x):
    hi = x.astype(BF16)
    lo = (x - hi.astype(F32)).astype(BF16)
    return hi, lo


def _dot(a, b):
    return jnp.dot(a, b, preferred_element_type=F32)


def _dot_nt(a, b):
    return lax.dot_general(a, b, (((1,), (1,)), ((), ())), preferred_element_type=F32)


def _dot_exact_rhs(m01, x):
    hi, mid, lo = _split3(x)
    return _dot(m01, hi) + _dot(m01, mid) + _dot(m01, lo)


def _dot_exact_lhs(x, m01):
    hi, mid, lo = _split3(x)
    return _dot(hi, m01) + _dot(mid, m01) + _dot(lo, m01)


def _silu(x):
    return x * jax.nn.sigmoid(x)


def _softplus(x):
    return jnp.maximum(x, 0.0) + jnp.log1p(jnp.exp(-jnp.abs(x)))


def _rmsnorm_rows(x, w):
    ms = jnp.mean(x * x, axis=-1, keepdims=True)
    return x * lax.rsqrt(ms + EPS) * w


def _inproj_kernel(col_slices, x_ref, lnw_ref, w_ref, *out_refs):
    hn = _rmsnorm_rows(x_ref[...], lnw_ref[...]).astype(BF16)
    for ref, (off, width) in zip(out_refs, col_slices):
        ref[...] = _dot(hn, w_ref[:, off:off + width]).astype(ref.dtype)


def _inproj_call(x2, lnw, w_pad, col_slices, tm):
    t, d = x2.shape
    out_shape = [jax.ShapeDtypeStruct((t, width), F32) for _, width in col_slices]
    out_specs = [pl.BlockSpec((tm, width), lambda i: (i, 0)) for _, width in col_slices]
    return pl.pallas_call(
        functools.partial(_inproj_kernel, col_slices),
        grid=(t // tm,),
        in_specs=[
            pl.BlockSpec((tm, d), lambda i: (i, 0)),
            pl.BlockSpec((1, d), lambda i: (0, 0)),
            pl.BlockSpec(w_pad.shape, lambda i: (0, 0), pipeline_mode=pl.Buffered(1)),
        ],
        out_specs=out_specs,
        out_shape=out_shape,
        compiler_params=_params(("parallel",)),
        name="inproj",
    )(x2, lnw, w_pad)


def _ssd_kernel(lc, z_ref, xbc_ref, dt_ref, convw_ref, convb_ref, dtb_ref, ahead_ref, dskip_ref,
                normw_ref, expand_ref, y_ref, xpad_ref, state_ref, ybuf_ref):
    inner = SSD_HEADS * SSD_HEAD_DIM
    gn = SSD_GROUPS * SSD_STATE
    pair_w = 2 * SSD_HEAD_DIM

    @pl.when(pl.program_id(1) == 0)
    def _():
        xpad_ref[0:CONV_PAD, :] = jnp.zeros((CONV_PAD, xpad_ref.shape[1]), F32)
        state_ref[...] = jnp.zeros(state_ref.shape, F32)

    xpad_ref[CONV_PAD:CONV_PAD + lc, :] = xbc_ref[...]
    conv = convb_ref[...]
    for k in range(SSD_CONV):
        start = CONV_PAD - (SSD_CONV - 1) + k
        conv = conv + convw_ref[k:k + 1, :] * xpad_ref[start:start + lc, :]
    xpad_ref[0:CONV_PAD, :] = xpad_ref[lc:lc + CONV_PAD, :]
    xbc = _silu(conv)
    xs = xbc[:, :inner]

    dt = _softplus(dt_ref[...] + dtb_ref[...])
    a = dt * ahead_ref[...]

    row = lax.broadcasted_iota(jnp.int32, (lc, lc), 0)
    col = lax.broadcasted_iota(jnp.int32, (lc, lc), 1)
    tril = row >= col
    tri_bf = tril.astype(BF16)
    acum = _dot_exact_rhs(tri_bf, a)
    acum_t = acum.T
    a_last = acum[lc - 1:lc, :]

    expand = expand_ref[...]
    dt_x = _dot_exact_lhs(dt, expand)
    from_start_x = _dot_exact_lhs(jnp.exp(acum), expand)
    to_end_x = _dot_exact_lhs(jnp.exp(a_last - acum), expand)
    chunk_decay = jnp.exp(a_last)
    xdt = xs * dt_x
    xw = xdt * to_end_x

    lane = lax.broadcasted_iota(jnp.int32, (lc, pair_w), 1)
    first_half = lane < SSD_HEAD_DIM

    heads_per_group = SSD_HEADS // SSD_GROUPS
    for g in range(SSD_GROUPS):
        bm = xbc[:, inner + g * SSD_STATE: inner + (g + 1) * SSD_STATE].astype(BF16)
        cm = xbc[:, inner + gn + g * SSD_STATE: inner + gn + (g + 1) * SSD_STATE].astype(BF16)
        cb = _dot_nt(cm, bm)
        for pr in range(heads_per_group // 2):
            pidx = g * (heads_per_group // 2) + pr
            lo, hi = pidx * pair_w, (pidx + 1) * pair_w
            xdt_p = xdt[:, lo:hi]
            y_pair = None
            for sub in range(2):
                h = 2 * pidx + sub
                diff = acum[:, h:h + 1] - acum_t[h:h + 1, :]
                decay = jnp.where(tril, jnp.exp(jnp.where(tril, diff, 0.0)), 0.0)
                m_h = (cb * decay).astype(BF16)
                sel = first_half if sub == 0 else jnp.logical_not(first_half)
                contrib = _dot(m_h, jnp.where(sel, xdt_p, 0.0).astype(BF16))
                y_pair = contrib if y_pair is None else y_pair + contrib
            st = state_ref[lo:hi, :]
            y_off = _dot_nt(cm, st.astype(BF16)) * from_start_x[:, lo:hi]
            ybuf_ref[:, lo:hi] = y_pair + y_off + xs[:, lo:hi] * dskip_ref[:, lo:hi]
            r_idx = lax.broadcasted_iota(jnp.int32, (pair_w, SSD_STATE), 0)
            dec = jnp.where(r_idx < SSD_HEAD_DIM, chunk_decay[:, 2 * pidx:2 * pidx + 1],
                            chunk_decay[:, 2 * pidx + 1:2 * pidx + 2])
            state_ref[lo:hi, :] = st * dec + _dot(xw[:, lo:hi].T.astype(BF16), bm)

    y = ybuf_ref[...] * _silu(z_ref[...])
    y_ref[...] = _rmsnorm_rows(y, normw_ref[...]).astype(y_ref.dtype)


def _ssd_call(z, xbc, dtp, convw, convb, dtb, ahead, dskip_x, normw, expand, bsz, lc):
    t, inner = z.shape
    cw = xbc.shape[1]
    nc = t // bsz // lc
    tok = lambda b, c: (b * nc + c, 0)
    const = lambda b, c: (0, 0)
    return pl.pallas_call(
        functools.partial(_ssd_kernel, lc),
        grid=(bsz, nc),
        in_specs=[
            pl.BlockSpec((lc, inner), tok),
            pl.BlockSpec((lc, cw), tok),
            pl.BlockSpec((lc, LANES), tok),
            pl.BlockSpec(convw.shape, const),
            pl.BlockSpec(convb.shape, const),
            pl.BlockSpec(dtb.shape, const),
            pl.BlockSpec(ahead.shape, const),
            pl.BlockSpec(dskip_x.shape, const),
            pl.BlockSpec(normw.shape, const),
            pl.BlockSpec(expand.shape, const),
        ],
        out_specs=pl.BlockSpec((lc, inner), tok),
        out_shape=jax.ShapeDtypeStruct((t, inner), BF16),
        scratch_shapes=[
            pltpu.VMEM((lc + CONV_PAD, cw), F32),
            pltpu.VMEM((inner, SSD_STATE), F32),
            pltpu.VMEM((lc, inner), F32),
        ],
        compiler_params=_params(("parallel", "arbitrary")),
        name="ssd",
    )(z, xbc, dtp, convw, convb, dtb, ahead, dskip_x, normw, expand)


def _hgrn2_kernel(lc, sb, q_ref, f_ref, i_ref, g_ref, lb_ref, normw_ref, y_ref, state_ref):
    nsb = lc // sb

    @pl.when(pl.program_id(1) == 0)
    def _():
        state_ref[...] = jnp.zeros(state_ref.shape, F32)

    lb = lb_ref[...]
    f = lb + (1.0 - lb) * jax.nn.sigmoid(f_ref[...])
    logf = jnp.log(f)
    kk = 1.0 - f
    q = _silu(q_ref[...])
    v = i_ref[...]
    v_bf = v.astype(BF16)

    row = lax.broadcasted_iota(jnp.int32, (lc, lc), 0)
    col = lax.broadcasted_iota(jnp.int32, (lc, lc), 1)
    tri_bf = (row >= col).astype(BF16)
    gcum = _dot_exact_rhs(tri_bf, logf)
    g_end = gcum[lc - 1:lc, :]
    k_end_bf = (kk * jnp.exp(g_end - gcum)).astype(BF16)
    state_decay = jnp.exp(g_end)

    qs_bf, q_inter_bf, k_ref_bf = [], [], []
    for ib in range(nsb):
        r0, r1 = ib * sb, (ib + 1) * sb
        if ib == 0:
            gref = jnp.zeros_like(g_end)
        else:
            gref = gcum[r0 - 1:r0, :]
        e_q = jnp.exp(gcum[r0:r1, :] - gref)
        qs = q[r0:r1, :] * e_q
        qs_bf.append(qs.astype(BF16))
        q_inter_bf.append((qs * jnp.exp(gref)).astype(BF16))
        k_ref_bf.append((kk[:r1, :] * jnp.exp(gref - gcum[:r1, :])).astype(BF16))

    for h in range(HG_HEADS):
        lo, hi = h * HG_DIM, (h + 1) * HG_DIM
        st_t = state_ref[lo:hi, :]
        st_bf = st_t.astype(BF16)
        v_h = v_bf[:, lo:hi]
        for ib in range(nsb):
            r0, r1 = ib * sb, (ib + 1) * sb
            att = _dot_nt(qs_bf[ib][:, lo:hi], k_ref_bf[ib][:, lo:hi])
            rr = lax.broadcasted_iota(jnp.int32, (sb, r1), 0) + r0
            cc = lax.broadcasted_iota(jnp.int32, (sb, r1), 1)
            att = jnp.where(rr >= cc, att, 0.0).astype(BF16)
            o = _dot(att, v_h[:r1, :]) + _dot_nt(q_inter_bf[ib][:, lo:hi], st_bf)
            gate = _silu(g_ref[r0:r1, lo:hi])
            y_ref[r0:r1, lo:hi] = (_rmsnorm_rows(o, normw_ref[:, lo:hi]) * gate).astype(y_ref.dtype)
        state_ref[lo:hi, :] = (st_t * state_decay[:, lo:hi]
                               + _dot(v[:, lo:hi].T.astype(BF16), k_end_bf[:, lo:hi]))


def _hgrn2_call(q, f, i, g, lb, normw, bsz, lc, sb):
    t, kd = q.shape
    nc = t // bsz // lc
    tok = lambda b, c: (b * nc + c, 0)
    const = lambda b, c: (0, 0)
    return pl.pallas_call(
        functools.partial(_hgrn2_kernel, lc, sb),
        grid=(bsz, nc),
        in_specs=[
            pl.BlockSpec((lc, kd), tok),
            pl.BlockSpec((lc, kd), tok),
            pl.BlockSpec((lc, kd), tok),
            pl.BlockSpec((lc, kd), tok),
            pl.BlockSpec(lb.shape, const),
            pl.BlockSpec(normw.shape, const),
        ],
        out_specs=pl.BlockSpec((lc, kd), tok),
        out_shape=jax.ShapeDtypeStruct((t, kd), BF16),
        scratch_shapes=[pltpu.VMEM((HG_HEADS * HG_DIM, HG_DIM), F32)],
        compiler_params=_params(("parallel", "arbitrary")),
        name="hgrn2",
    )(q, f, i, g, lb, normw)


def _outproj_kernel(x_ref, yssd_ref, yhg_ref, wout_ref, lnw_ref, wq_hi_ref, wq_lo_ref,
                    keys_hi_ref, keys_lo_ref, h_ref, hnt_ref, st_ref):
    inner = yssd_ref.shape[1]
    h = x_ref[...] + _dot(yssd_ref[...], wout_ref[:inner, :]) + _dot(yhg_ref[...], wout_ref[inner:, :])
    h_ref[...] = h
    hn_t = _rmsnorm_rows(h, lnw_ref[...]).T
    x_hi, x_lo = _split2(hn_t)
    hnt_ref[...] = x_hi
    wq_hi = wq_hi_ref[...]
    q_t = _dot(wq_hi, x_hi) + _dot(wq_hi, x_lo) + _dot(wq_lo_ref[...], x_hi)
    n_hp = keys_hi_ref.shape[0]
    half = keys_hi_ref.shape[2]
    for hp in range(n_hp):
        q_hi, q_lo = _split2(q_t[hp * half:(hp + 1) * half, :])
        k_hi = keys_hi_ref[hp]
        st_ref[hp] = _dot(k_hi, q_hi) + _dot(k_hi, q_lo) + _dot(keys_lo_ref[hp], q_hi)


def _outproj_call(x2, y_ssd, y_hg, w_out, lnw, wq_hi, wq_lo, keys_hi, keys_lo, tm):
    t, d = x2.shape
    n_hp, n_keys, half = keys_hi.shape
    const2 = lambda i: (0, 0)
    const3 = lambda i: (0, 0, 0)
    one = pl.Buffered(1)
    return pl.pallas_call(
        _outproj_kernel,
        grid=(t // tm,),
        in_specs=[
            pl.BlockSpec((tm, d), lambda i: (i, 0)),
            pl.BlockSpec((tm, y_ssd.shape[1]), lambda i: (i, 0)),
            pl.BlockSpec((tm, y_hg.shape[1]), lambda i: (i, 0)),
            pl.BlockSpec(w_out.shape, const2, pipeline_mode=one),
            pl.BlockSpec(lnw.shape, const2),
            pl.BlockSpec(wq_hi.shape, const2, pipeline_mode=one),
            pl.BlockSpec(wq_lo.shape, const2, pipeline_mode=one),
            pl.BlockSpec(keys_hi.shape, const3, pipeline_mode=one),
            pl.BlockSpec(keys_lo.shape, const3, pipeline_mode=one),
        ],
        out_specs=[
            pl.BlockSpec((tm, d), lambda i: (i, 0)),
            pl.BlockSpec((d, tm), lambda i: (0, i)),
            pl.BlockSpec((n_hp, n_keys, tm), lambda i: (0, 0, i)),
        ],
        out_shape=[
            jax.ShapeDtypeStruct((t, d), F32),
            jax.ShapeDtypeStruct((d, t), BF16),
            jax.ShapeDtypeStruct((n_hp, n_keys, t), F32),
        ],
        compiler_params=_params(("parallel",)),
        name="outproj",
    )(x2, y_ssd, y_hg, w_out, lnw, wq_hi, wq_lo, keys_hi, keys_lo)


NO_RANK = 255.0


def _extract_top(v, n_take, out_ref, want_rank):
    rank = jnp.full(v.shape, NO_RANK, F32) if want_rank else None
    for r in range(n_take):
        m = jnp.max(v, axis=0, keepdims=True)
        out_ref[r:r + 1, :] = m
        if want_rank or r + 1 < n_take:
            hit = v == m
            if want_rank:
                rank = jnp.where(hit, np.float32(r), rank)
            if r + 1 < n_take:
                v = jnp.where(hit, -jnp.inf, v)
    return rank


def _topk_kernel(st_ref, r2_ref, e2_ref, n1_ref, a1_ref, t1_ref, t2_ref, tc_ref):
    n_keys = st_ref.shape[1]
    tk = st_ref.shape[2]
    k = PEER_TOPK
    k_iota = lax.broadcasted_iota(jnp.int32, (k, LANES), 0)

    def head_strip(idx, carry):
        s = idx // PEER_HEADS
        h = idx % PEER_HEADS
        ls = pl.ds(pl.multiple_of(s * LANES, LANES), LANES)
        s1 = st_ref[2 * h, :, ls]
        s2 = st_ref[2 * h + 1, :, ls]
        _extract_top(s1, k, t1_ref, False)
        rank2 = _extract_top(s2, k, t2_ref, True)
        t1 = t1_ref[...]
        t2 = t2_ref[...]
        tiles = [t1[0:1, :] + t2]
        for a in range(1, k // 2):
            nb = k // (a + 1)
            tiles.append(jnp.where(k_iota[:SUBLANES] < nb, t1[a:a + 1, :] + t2[:SUBLANES, :],
                                   -jnp.inf))
        tiles.append(t1[k // 2:, :] + t2[0:1, :])
        cand = jnp.concatenate(tiles, axis=0)
        _extract_top(cand, k, tc_ref, False)
        top = tc_ref[...]
        cmax = top[0:1, :]
        z = jnp.sum(jnp.exp(top - cmax), axis=0, keepdims=True)
        tau = top[k - 1:k, :]
        passed = jnp.where(jnp.logical_and(cand >= tau, cand > -jnp.inf), 1.0, 0.0)
        counts = [jnp.sum(passed[0:k, :], axis=0, keepdims=True)]
        for a in range(1, k // 2):
            r0 = k + (a - 1) * SUBLANES
            counts.append(jnp.sum(passed[r0:r0 + SUBLANES, :], axis=0, keepdims=True))
        r0 = k + (k // 2 - 1) * SUBLANES
        counts += [passed[r0 + a:r0 + a + 1, :] for a in range(k - k // 2)]
        n_row = jnp.zeros((n_keys, LANES), F32)
        for a in range(k):
            n_row = jnp.where(s1 == t1[a:a + 1, :], counts[a], n_row)
        r2_ref[h, :, ls] = rank2
        e2_ref[h, :, ls] = jnp.exp(s2 - t2[0:1, :])
        n1_ref[h, :, ls] = n_row
        a1_ref[h, :, ls] = jnp.exp(s1 - t1[0:1, :]) / z
        return carry

    lax.fori_loop(0, (tk // LANES) * PEER_HEADS, head_strip, 0)


def _topk_call(st, tk):
    n_hp, n_keys, t = st.shape
    nh = n_hp // 2
    blk = pl.BlockSpec((nh, n_keys, tk), lambda i: (0, 0, i))
    per_key = jax.ShapeDtypeStruct((nh, n_keys, t), F32)
    return pl.pallas_call(
        _topk_kernel,
        grid=(t // tk,),
        in_specs=[pl.BlockSpec((n_hp, n_keys, tk), lambda i: (0, 0, i))],
        out_specs=[blk, blk, blk, blk],
        out_shape=[per_key, per_key, per_key, per_key],
        scratch_shapes=[
            pltpu.VMEM((PEER_TOPK, LANES), F32),
            pltpu.VMEM((PEER_TOPK, LANES), F32),
            pltpu.VMEM((PEER_TOPK, LANES), F32),
        ],
        compiler_params=_params(("parallel",)),
        name="peer_topk",
    )(st)


BF16_ROWS = 2 * SUBLANES
MM_ROWS = 256


def _peer_kernel(ib, hnt_ref, r2_ref, e2_ref, n1_ref, a1_ref, u_ref, vt_ref, h_ref, lnf_ref,
                 out_ref, acc_ref, s_ref, hid_ref, r2b_ref, e2b_ref):
    step = pl.program_id(1)
    n_keys = r2_ref.shape[1]
    tt = hnt_ref.shape[1]
    sqrt_half = np.float32(np.sqrt(0.5))
    per_chunk = MM_ROWS // n_keys
    n_chunks = ib // per_chunk

    @pl.when(step == 0)
    def _():
        acc_ref[...] = jnp.zeros(acc_ref.shape, F32)
        for h in range(PEER_HEADS):
            r2b_ref[h] = r2_ref[h].astype(BF16)
            e2b_ref[h] = e2_ref[h].astype(BF16)

    for c in range(n_chunks):
        rows = slice(c * MM_ROWS, (c + 1) * MM_ROWS)
        s_ref[rows, :] = _dot(u_ref[rows, :], hnt_ref[...])

    def bcast_rows(ref, h, ii, ls):
        return jnp.broadcast_to(ref[h, ii:ii + 1, ls], (BF16_ROWS, LANES)).astype(BF16)

    zero = jnp.zeros((BF16_ROWS, LANES), BF16)
    for c in range(n_chunks):
        for ii in range(c * per_chunk, (c + 1) * per_chunk):
            for s in range(tt // LANES):
                ls = slice(s * LANES, (s + 1) * LANES)
                n_rows = [bcast_rows(n1_ref, h, ii, ls) for h in range(PEER_HEADS)]
                a_rows = [bcast_rows(a1_ref, h, ii, ls) for h in range(PEER_HEADS)]
                for jt in range(n_keys // BF16_ROWS):
                    kr = slice(jt * BF16_ROWS, (jt + 1) * BF16_ROWS)
                    gate = None
                    for h in range(PEER_HEADS):
                        term = jnp.where(r2b_ref[h, kr, ls] < n_rows[h], e2b_ref[h, kr, ls], zero)
                        term = term * a_rows[h]
                        gate = term if gate is None else gate + term
                    er = slice(ii * n_keys + jt * BF16_ROWS, ii * n_keys + (jt + 1) * BF16_ROWS)
                    pre = s_ref[er, ls]
                    act = 0.5 * pre * (1.0 + lax.erf(pre * sqrt_half))
                    hid_ref[er, ls] = act.astype(BF16) * gate

    upd = None
    for c in range(n_chunks):
        rows = slice(c * MM_ROWS, (c + 1) * MM_ROWS)
        part = _dot(vt_ref[:, rows], hid_ref[rows, :])
        upd = part if upd is None else upd + part
    acc_ref[...] += upd

    @pl.when(step == pl.num_programs(1) - 1)
    def _():
        total = h_ref[...] + acc_ref[...].T
        out_ref[...] = _rmsnorm_rows(total, lnf_ref[...])


def _peer_call(hnt, r2, e2, n1, a1, u_bf, vt_bf, h, lnf, tt, ib):
    d, t = hnt.shape
    nh, n_keys, _ = r2.shape
    n_exp = u_bf.shape[0]
    eb = ib * n_keys
    assert eb % MM_ROWS == 0 and MM_ROWS % n_keys == 0
    full = pl.BlockSpec((nh, n_keys, tt), lambda i, j: (0, 0, i))
    rows = pl.BlockSpec((nh, ib, tt), lambda i, j: (0, j, i))
    return pl.pallas_call(
        functools.partial(_peer_kernel, ib),
        grid=(t // tt, n_exp // eb),
        in_specs=[
            pl.BlockSpec((d, tt), lambda i, j: (0, i)),
            full, full, rows, rows,
            pl.BlockSpec((eb, d), lambda i, j: (j, 0)),
            pl.BlockSpec((d, eb), lambda i, j: (0, j)),
            pl.BlockSpec((tt, d), lambda i, j: (i, 0)),
            pl.BlockSpec((1, d), lambda i, j: (0, 0)),
        ],
        out_specs=pl.BlockSpec((tt, d), lambda i, j: (i, 0)),
        out_shape=jax.ShapeDtypeStruct((t, d), F32),
        scratch_shapes=[
            pltpu.VMEM((d, tt), F32),
            pltpu.VMEM((eb, tt), F32),
            pltpu.VMEM((eb, tt), BF16),
            pltpu.VMEM((nh, n_keys, tt), BF16),
            pltpu.VMEM((nh, n_keys, tt), BF16),
        ],
        compiler_params=_params(("parallel", "arbitrary")),
        name="peer_experts",
    )(hnt, r2, e2, n1, a1, u_bf, vt_bf, h, lnf)


def _layer(h2, bsz, lb, ln_mix_w, w_in, conv_w, conv_b, dt_bias, a_log, d_skip, ssd_norm_w,
           hg_norm_w, w_out, ln_ffn_w, w_query, sub_keys, u_table, v_table, ln_out_w,
           tm, lc_ssd, lc_hg, sb_hg, tk, tt, ib):
    t, d = h2.shape
    inner = SSD_HEADS * SSD_HEAD_DIM
    conv_ch = inner + 2 * SSD_GROUPS * SSD_STATE
    hgk = HG_HEADS * HG_DIM

    sizes = (inner, conv_ch, SSD_HEADS, hgk, hgk, hgk, hgk)
    offs = np.concatenate([[0], np.cumsum(sizes)])
    wz, wxbc, wdt, wq_, wf, wi, wg = [w_in[:, offs[n]:offs[n + 1]] for n in range(7)]
    wdt_pad = jnp.pad(wdt, ((0, 0), (0, LANES - SSD_HEADS)))
    w_pad = jnp.concatenate([wz, wxbc, wq_, wf, wi, wg, wdt_pad], axis=1).astype(BF16)
    widths = (inner, conv_ch, hgk, hgk, hgk, hgk, LANES)
    starts = np.concatenate([[0], np.cumsum(widths)])[:-1]
    col_slices = tuple((int(s), int(w)) for s, w in zip(starts, widths))

    z, xbc, q_raw, f_raw, i_raw, g_raw, dt_raw = _inproj_call(
        h2, ln_mix_w.reshape(1, d), w_pad, col_slices, tm)

    pad_h = (0, LANES - SSD_HEADS)
    dtb = jnp.pad(dt_bias.astype(F32), pad_h).reshape(1, LANES)
    ahead = jnp.pad(-jnp.exp(a_log.astype(F32)), pad_h).reshape(1, LANES)
    dskip_x = jnp.repeat(d_skip.astype(F32), SSD_HEAD_DIM).reshape(1, inner)
    expand = (jnp.arange(LANES)[:, None] == (jnp.arange(inner)[None, :] // SSD_HEAD_DIM)).astype(BF16)
    y_ssd = _ssd_call(z, xbc, dt_raw, conv_w.astype(F32), conv_b.reshape(1, conv_ch).astype(F32),
                      dtb, ahead, dskip_x, ssd_norm_w.reshape(1, inner), expand, bsz, lc_ssd)

    y_hg = _hgrn2_call(q_raw, f_raw, i_raw, g_raw, lb.reshape(1, hgk), hg_norm_w.reshape(1, hgk),
                       bsz, lc_hg, sb_hg)

    wq_hi, wq_lo = _split2(w_query.T.astype(F32))
    n_hp = PEER_HEADS * 2
    half = sub_keys.shape[-1]
    keys_hi, keys_lo = _split2(sub_keys.reshape(n_hp, PEER_KEYS, half).astype(F32))
    h_res, hn_t, st = _outproj_call(h2, y_ssd, y_hg, w_out.astype(BF16), ln_ffn_w.reshape(1, d),
                                    wq_hi, wq_lo, keys_hi, keys_lo, tm)

    r2, e2, n1, a1 = _topk_call(st, tk)

    return _peer_call(hn_t, r2, e2, n1, a1, u_table.astype(BF16), v_table.T.astype(BF16),
                      h_res, ln_out_w.reshape(1, d), tt, ib)


def kernel(x, ln_mix_w, w_in, conv_w, conv_b, dt_bias, a_log, d_skip, ssd_norm_w, lb_logits,
           hg_norm_w, w_out, ln_ffn_w, w_query, sub_keys, u_table, v_table, ln_final_w):
    bsz, seqlen, d = x.shape
    depth = w_in.shape[0]
    assert depth == 1, "the final RMSNorm is fused into the (single) layer's expert kernel"
    lb_all = jnp.cumsum(jax.nn.softmax(lb_logits.astype(F32), axis=0), axis=0)
    t = bsz * seqlen
    tm = min(256, t)
    lc = min(128, seqlen)
    out = _layer(x.reshape(t, d), bsz, lb_all[0], ln_mix_w[0], w_in[0], conv_w[0], conv_b[0],
                 dt_bias[0], a_log[0], d_skip[0], ssd_norm_w[0], hg_norm_w[0], w_out[0],
                 ln_ffn_w[0], w_query[0], sub_keys[0], u_table[0], v_table[0], ln_final_w,
                 tm=tm, lc_ssd=lc, lc_hg=min(64, seqlen), sb_hg=32, tk=min(512, t),
                 tt=min(512, t), ib=SUBLANES)
    return out.reshape(bsz, seqlen, d)
```

```python
import functools

import numpy as np
import jax
import jax.numpy as jnp
from jax import lax
from jax.experimental import pallas as pl
from jax.experimental.pallas import tpu as pltpu

F32 = jnp.float32
BF16 = jnp.bfloat16
EPS = 1e-6

SSD_HEADS = 16
SSD_HEAD_DIM = 64
SSD_GROUPS = 4
SSD_STATE = 128
SSD_CONV = 4
HG_HEADS = 8
HG_DIM = 128
PEER_HEADS = 8
PEER_KEYS = 128
PEER_TOPK = 16

LANES = 128
SUBLANES = 8
VMEM_LIMIT = 56 * 1024 * 1024

CONV_PAD = SUBLANES


def _params(sem):
    return pltpu.CompilerParams(dimension_semantics=sem, vmem_limit_bytes=VMEM_LIMIT)


def _split3(x):
    hi = x.astype(BF16)
    r1 = x - hi.astype(F32)
    mid = r1.astype(BF16)
    lo = (r1 - mid.astype(F32)).astype(BF16)
    return hi, mid, lo


def _split2(x):
    hi = x.astype(BF16)
    lo = (x - hi.astype(F32)).astype(BF16)
    return hi, lo


def _dot(a, b):
    return jnp.dot(a, b, preferred_element_type=F32)


def _dot_nt(a, b):
    return lax.dot_general(a, b, (((1,), (1,)), ((), ())), preferred_element_type=F32)


def _dot_exact_rhs(m01, x):
    hi, mid, lo = _split3(x)
    return _dot(m01, hi) + _dot(m01, mid) + _dot(m01, lo)


def _dot_exact_lhs(x, m01):
    hi, mid, lo = _split3(x)
    return _dot(hi, m01) + _dot(mid, m01) + _dot(lo, m01)


def _silu(x):
    return x * jax.nn.sigmoid(x)


def _softplus(x):
    return jnp.maximum(x, 0.0) + jnp.log1p(jnp.exp(-jnp.abs(x)))


def _rmsnorm_rows(x, w):
    ms = jnp.mean(x * x, axis=-1, keepdims=True)
    return x * lax.rsqrt(ms + EPS) * w


def _inproj_kernel(col_slices, x_ref, lnw_ref, w_ref, *out_refs):
    hn = _rmsnorm_rows(x_ref[...], lnw_ref[...]).astype(BF16)
    for ref, (off, width) in zip(out_refs, col_slices):
        ref[...] = _dot(hn, w_ref[:, off:off + width]).astype(ref.dtype)


def _inproj_call(x2, lnw, w_pad, col_slices, tm):
    t, d = x2.shape
    out_shape = [jax.ShapeDtypeStruct((t, width), F32) for _, width in col_slices]
    out_specs = [pl.BlockSpec((tm, width), lambda i: (i, 0)) for _, width in col_slices]
    return pl.pallas_call(
        functools.partial(_inproj_kernel, col_slices),
        grid=(t // tm,),
        in_specs=[
            pl.BlockSpec((tm, d), lambda i: (i, 0)),
            pl.BlockSpec((1, d), lambda i: (0, 0)),
            pl.BlockSpec(w_pad.shape, lambda i: (0, 0), pipeline_mode=pl.Buffered(1)),
        ],
        out_specs=out_specs,
        out_shape=out_shape,
        compiler_params=_params(("parallel",)),
        name="inproj",
    )(x2, lnw, w_pad)


def _ssd_kernel(lc, z_ref, xbc_ref, dt_ref, convw_ref, convb_ref, dtb_ref, ahead_ref, dskip_ref,
                normw_ref, expand_ref, y_ref, xpad_ref, state_ref, ybuf_ref):
    inner = SSD_HEADS * SSD_HEAD_DIM
    gn = SSD_GROUPS * SSD_STATE
    pair_w = 2 * SSD_HEAD_DIM

    @pl.when(pl.program_id(1) == 0)
    def _():
        xpad_ref[0:CONV_PAD, :] = jnp.zeros((CONV_PAD, xpad_ref.shape[1]), F32)
        state_ref[...] = jnp.zeros(state_ref.shape, F32)

    xpad_ref[CONV_PAD:CONV_PAD + lc, :] = xbc_ref[...]
    conv = convb_ref[...]
    for k in range(SSD_CONV):
        start = CONV_PAD - (SSD_CONV - 1) + k
        conv = conv + convw_ref[k:k + 1, :] * xpad_ref[start:start + lc, :]
    xpad_ref[0:CONV_PAD, :] = xpad_ref[lc:lc + CONV_PAD, :]
    xbc = _silu(conv)
    xs = xbc[:, :inner]

    dt = _softplus(dt_ref[...] + dtb_ref[...])
    a = dt * ahead_ref[...]

    row = lax.broadcasted_iota(jnp.int32, (lc, lc), 0)
    col = lax.broadcasted_iota(jnp.int32, (lc, lc), 1)
    tril = row >= col
    tri_bf = tril.astype(BF16)
    acum = _dot_exact_rhs(tri_bf, a)
    acum_t = acum.T
    a_last = acum[lc - 1:lc, :]

    expand = expand_ref[...]
    dt_x = _dot_exact_lhs(dt, expand)
    from_start_x = _dot_exact_lhs(jnp.exp(acum), expand)
    to_end_x = _dot_exact_lhs(jnp.exp(a_last - acum), expand)
    chunk_decay = jnp.exp(a_last)
    xdt = xs * dt_x
    xw = xdt * to_end_x

    lane = lax.broadcasted_iota(jnp.int32, (lc, pair_w), 1)
    first_half = lane < SSD_HEAD_DIM

    heads_per_group = SSD_HEADS // SSD_GROUPS
    for g in range(SSD_GROUPS):
        bm = xbc[:, inner + g * SSD_STATE: inner + (g + 1) * SSD_STATE].astype(BF16)
        cm = xbc[:, inner + gn + g * SSD_STATE: inner + gn + (g + 1) * SSD_STATE].astype(BF16)
        cb = _dot_nt(cm, bm)
        for pr in range(heads_per_group // 2):
            pidx = g * (heads_per_group // 2) + pr
            lo, hi = pidx * pair_w, (pidx + 1) * pair_w
            xdt_p = xdt[:, lo:hi]
            y_pair = None
            for sub in range(2):
                h = 2 * pidx + sub
                diff = acum[:, h:h + 1] - acum_t[h:h + 1, :]
                decay = jnp.where(tril, jnp.exp(jnp.where(tril, diff, 0.0)), 0.0)
                m_h = (cb * decay).astype(BF16)
                sel = first_half if sub == 0 else jnp.logical_not(first_half)
                contrib = _dot(m_h, jnp.where(sel, xdt_p, 0.0).astype(BF16))
                y_pair = contrib if y_pair is None else y_pair + contrib
            st = state_ref[lo:hi, :]
            y_off = _dot_nt(cm, st.astype(BF16)) * from_start_x[:, lo:hi]
            ybuf_ref[:, lo:hi] = y_pair + y_off + xs[:, lo:hi] * dskip_ref[:, lo:hi]
            r_idx = lax.broadcasted_iota(jnp.int32, (pair_w, SSD_STATE), 0)
            dec = jnp.where(r_idx < SSD_HEAD_DIM, chunk_decay[:, 2 * pidx:2 * pidx + 1],
                            chunk_decay[:, 2 * pidx + 1:2 * pidx + 2])
            state_ref[lo:hi, :] = st * dec + _dot(xw[:, lo:hi].T.astype(BF16), bm)

    y = ybuf_ref[...] * _silu(z_ref[...])
    y_ref[...] = _rmsnorm_rows(y, normw_ref[...]).astype(y_ref.dtype)


def _ssd_call(z, xbc, dtp, convw, convb, dtb, ahead, dskip_x, normw, expand, bsz, lc):
    t, inner = z.shape
    cw = xbc.shape[1]
    nc = t // bsz // lc
    tok = lambda b, c: (b * nc + c, 0)
    const = lambda b, c: (0, 0)
    return pl.pallas_call(
        functools.partial(_ssd_kernel, lc),
        grid=(bsz, nc),
        in_specs=[
            pl.BlockSpec((lc, inner), tok),
            pl.BlockSpec((lc, cw), tok),
            pl.BlockSpec((lc, LANES), tok),
            pl.BlockSpec(convw.shape, const),
            pl.BlockSpec(convb.shape, const),
            pl.BlockSpec(dtb.shape, const),
            pl.BlockSpec(ahead.shape, const),
            pl.BlockSpec(dskip_x.shape, const),
            pl.BlockSpec(normw.shape, const),
            pl.BlockSpec(expand.shape, const),
        ],
        out_specs=pl.BlockSpec((lc, inner), tok),
        out_shape=jax.ShapeDtypeStruct((t, inner), BF16),
        scratch_shapes=[
            pltpu.VMEM((lc + CONV_PAD, cw), F32),
            pltpu.VMEM((inner, SSD_STATE), F32),
            pltpu.VMEM((lc, inner), F32),
        ],
        compiler_params=_params(("parallel", "arbitrary")),
        name="ssd",
    )(z, xbc, dtp, convw, convb, dtb, ahead, dskip_x, normw, expand)


def _hgrn2_kernel(lc, sb, q_ref, f_ref, i_ref, g_ref, lb_ref, normw_ref, y_ref, state_ref):
    nsb = lc // sb

    @pl.when(pl.program_id(1) == 0)
    def _():
        state_ref[...] = jnp.zeros(state_ref.shape, F32)

    lb = lb_ref[...]
    f = lb + (1.0 - lb) * jax.nn.sigmoid(f_ref[...])
    logf = jnp.log(f)
    kk = 1.0 - f
    q = _silu(q_ref[...])
    v = i_ref[...]
    v_bf = v.astype(BF16)

    row = lax.broadcasted_iota(jnp.int32, (lc, lc), 0)
    col = lax.broadcasted_iota(jnp.int32, (lc, lc), 1)
    tri_bf = (row >= col).astype(BF16)
    gcum = _dot_exact_rhs(tri_bf, logf)
    g_end = gcum[lc - 1:lc, :]
    k_end_bf = (kk * jnp.exp(g_end - gcum)).astype(BF16)
    state_decay = jnp.exp(g_end)

    q_inter_bf = (q * jnp.exp(gcum)).astype(BF16)
    key_row = lax.broadcasted_iota(jnp.int32, kk.shape, 0)
    qs_bf, k_ref_bf = [], []
    for ib in range(nsb):
        r0, r1 = ib * sb, (ib + 1) * sb
        if ib == 0:
            gref = jnp.zeros_like(g_end)
        else:
            gref = gcum[r0 - 1:r0, :]
        qs_bf.append((q[r0:r1, :] * jnp.exp(gcum[r0:r1, :] - gref)).astype(BF16))
        k_ref = kk * jnp.exp(gref - gcum)
        if r1 < lc:
            k_ref = jnp.where(key_row < r1, k_ref, 0.0)
        k_ref_bf.append(k_ref.astype(BF16))

    causal = row >= col
    heads = [(h * HG_DIM, (h + 1) * HG_DIM) for h in range(HG_HEADS)]
    states = [state_ref[lo:hi, :] for lo, hi in heads]
    o_inter = [_dot_nt(q_inter_bf[:, lo:hi], st.astype(BF16)) for (lo, hi), st in zip(heads, states)]
    for (lo, hi), st in zip(heads, states):
        state_ref[lo:hi, :] = (st * state_decay[:, lo:hi]
                               + _dot(v[:, lo:hi].T.astype(BF16), k_end_bf[:, lo:hi]))
    atts = [jnp.concatenate([_dot_nt(qs_bf[ib][:, lo:hi], k_ref_bf[ib][:, lo:hi])
                             for ib in range(nsb)], axis=0) for lo, hi in heads]
    atts = [jnp.where(causal, att, 0.0).astype(BF16) for att in atts]
    outs = [_dot(att, v_bf[:, lo:hi]) + oi for att, oi, (lo, hi) in zip(atts, o_inter, heads)]
    for o, (lo, hi) in zip(outs, heads):
        gate = _silu(g_ref[:, lo:hi])
        y_ref[:, lo:hi] = (_rmsnorm_rows(o, normw_ref[:, lo:hi]) * gate).astype(y_ref.dtype)


def _hgrn2_call(q, f, i, g, lb, normw, bsz, lc, sb):
    t, kd = q.shape
    nc = t // bsz // lc
    tok = lambda b, c: (b * nc + c, 0)
    const = lambda b, c: (0, 0)
    return pl.pallas_call(
        functools.partial(_hgrn2_kernel, lc, sb),
        grid=(bsz, nc),
        in_specs=[
            pl.BlockSpec((lc, kd), tok),
            pl.BlockSpec((lc, kd), tok),
            pl.BlockSpec((lc, kd), tok),
            pl.BlockSpec((lc, kd), tok),
            pl.BlockSpec(lb.shape, const),
            pl.BlockSpec(normw.shape, const),
        ],
        out_specs=pl.BlockSpec((lc, kd), tok),
        out_shape=jax.ShapeDtypeStruct((t, kd), BF16),
        scratch_shapes=[pltpu.VMEM((HG_HEADS * HG_DIM, HG_DIM), F32)],
        compiler_params=_params(("parallel", "arbitrary")),
        name="hgrn2",
    )(q, f, i, g, lb, normw)


def _outproj_kernel(x_ref, yssd_ref, yhg_ref, wout_ref, lnw_ref, wq_ref, keys_ref,
                    h_ref, hnt_ref, st_ref):
    inner = yssd_ref.shape[1]
    h = x_ref[...] + _dot(yssd_ref[...], wout_ref[:inner, :]) + _dot(yhg_ref[...], wout_ref[inner:, :])
    h_ref[...] = h
    hn_t = _rmsnorm_rows(h, lnw_ref[...]).T.astype(BF16)
    hnt_ref[...] = hn_t
    q_t = _dot(wq_ref[...], hn_t)
    n_hp = keys_ref.shape[0]
    half = keys_ref.shape[2]
    for hp in range(n_hp):
        st_ref[hp] = _dot(keys_ref[hp], q_t[hp * half:(hp + 1) * half, :].astype(BF16))


def _outproj_call(x2, y_ssd, y_hg, w_out, lnw, wq, keys, tm):
    t, d = x2.shape
    n_hp, n_keys, half = keys.shape
    const2 = lambda i: (0, 0)
    const3 = lambda i: (0, 0, 0)
    one = pl.Buffered(1)
    return pl.pallas_call(
        _outproj_kernel,
        grid=(t // tm,),
        in_specs=[
            pl.BlockSpec((tm, d), lambda i: (i, 0)),
            pl.BlockSpec((tm, y_ssd.shape[1]), lambda i: (i, 0)),
            pl.BlockSpec((tm, y_hg.shape[1]), lambda i: (i, 0)),
            pl.BlockSpec(w_out.shape, const2, pipeline_mode=one),
            pl.BlockSpec(lnw.shape, const2),
            pl.BlockSpec(wq.shape, const2, pipeline_mode=one),
            pl.BlockSpec(keys.shape, const3, pipeline_mode=one),
        ],
        out_specs=[
            pl.BlockSpec((tm, d), lambda i: (i, 0)),
            pl.BlockSpec((d, tm), lambda i: (0, i)),
            pl.BlockSpec((n_hp, n_keys, tm), lambda i: (0, 0, i)),
        ],
        out_shape=[
            jax.ShapeDtypeStruct((t, d), F32),
            jax.ShapeDtypeStruct((d, t), BF16),
            jax.ShapeDtypeStruct((n_hp, n_keys, t), F32),
        ],
        compiler_params=_params(("parallel",)),
        name="outproj",
    )(x2, y_ssd, y_hg, w_out, lnw, wq, keys)


NO_RANK = 255.0


def _extract_top(v, n_take, out_ref, want_rank):
    rank = jnp.full(v.shape, NO_RANK, F32) if want_rank else None
    for r in range(n_take):
        m = jnp.max(v, axis=0, keepdims=True)
        out_ref[r:r + 1, :] = m
        if want_rank or r + 1 < n_take:
            hit = v == m
            if want_rank:
                rank = jnp.where(hit, np.float32(r), rank)
            if r + 1 < n_take:
                v = jnp.where(hit, -jnp.inf, v)
    return rank


def _topk_kernel(st_ref, r2_ref, e2_ref, n1_ref, a1_ref, t1_ref, t2_ref, tc_ref):
    n_keys = st_ref.shape[1]
    tk = st_ref.shape[2]
    k = PEER_TOPK
    k_iota = lax.broadcasted_iota(jnp.int32, (k, LANES), 0)

    def head_strip(idx, carry):
        s = idx // PEER_HEADS
        h = idx % PEER_HEADS
        ls = pl.ds(pl.multiple_of(s * LANES, LANES), LANES)
        s1 = st_ref[2 * h, :, ls]
        s2 = st_ref[2 * h + 1, :, ls]
        _extract_top(s1, k, t1_ref, False)
        rank2 = _extract_top(s2, k, t2_ref, True)
        t1 = t1_ref[...]
        t2 = t2_ref[...]
        tiles = [t1[0:1, :] + t2]
        for a in range(1, k // 2):
            nb = k // (a + 1)
            tiles.append(jnp.where(k_iota[:SUBLANES] < nb, t1[a:a + 1, :] + t2[:SUBLANES, :],
                                   -jnp.inf))
        tiles.append(t1[k // 2:, :] + t2[0:1, :])
        cand = jnp.concatenate(tiles, axis=0)
        _extract_top(cand, k, tc_ref, False)
        top = tc_ref[...]
        cmax = top[0:1, :]
        z = jnp.sum(jnp.exp(top - cmax), axis=0, keepdims=True)
        tau = top[k - 1:k, :]
        passed = jnp.where(jnp.logical_and(cand >= tau, cand > -jnp.inf), 1.0, 0.0)
        counts = [jnp.sum(passed[0:k, :], axis=0, keepdims=True)]
        for a in range(1, k // 2):
            r0 = k + (a - 1) * SUBLANES
            counts.append(jnp.sum(passed[r0:r0 + SUBLANES, :], axis=0, keepdims=True))
        r0 = k + (k // 2 - 1) * SUBLANES
        counts += [passed[r0 + a:r0 + a + 1, :] for a in range(k - k // 2)]
        n_row = jnp.zeros((n_keys, LANES), F32)
        for a in range(k):
            n_row = jnp.where(s1 == t1[a:a + 1, :], counts[a], n_row)
        r2_ref[h, :, ls] = rank2
        e2_ref[h, :, ls] = jnp.exp(s2 - t2[0:1, :])
        n1_ref[h, :, ls] = n_row
        a1_ref[h, :, ls] = jnp.exp(s1 - t1[0:1, :]) / z
        return carry

    lax.fori_loop(0, (tk // LANES) * PEER_HEADS, head_strip, 0)


def _topk_call(st, tk):
    n_hp, n_keys, t = st.shape
    nh = n_hp // 2
    blk = pl.BlockSpec((nh, n_keys, tk), lambda i: (0, 0, i))
    per_key = jax.ShapeDtypeStruct((nh, n_keys, t), F32)
    return pl.pallas_call(
        _topk_kernel,
        grid=(t // tk,),
        in_specs=[pl.BlockSpec((n_hp, n_keys, tk), lambda i: (0, 0, i))],
        out_specs=[blk, blk, blk, blk],
        out_shape=[per_key, per_key, per_key, per_key],
        scratch_shapes=[
            pltpu.VMEM((PEER_TOPK, LANES), F32),
            pltpu.VMEM((PEER_TOPK, LANES), F32),
            pltpu.VMEM((PEER_TOPK, LANES), F32),
        ],
        compiler_params=_params(("parallel",)),
        name="peer_topk",
    )(st)


BF16_ROWS = 2 * SUBLANES


def _peer_kernel(ib, hnt_ref, r2_ref, e2_ref, n1_ref, a1_ref, u_ref, vt_ref, h_ref, lnf_ref,
                 out_ref, acc_ref, s_ref, hid_ref, tab_ref):
    step = pl.program_id(1)
    n_keys = r2_ref.shape[1]
    tt = hnt_ref.shape[1]
    sqrt_half = np.float32(np.sqrt(0.5))

    @pl.when(step == 0)
    def _():
        acc_ref[...] = jnp.zeros(acc_ref.shape, F32)
        for h in range(PEER_HEADS):
            tab_ref[2 * h, :n_keys, :tt] = r2_ref[h].astype(BF16)
            tab_ref[2 * h + 1, :n_keys, :tt] = e2_ref[h].astype(BF16)

    s_ref[:, :tt] = _dot(u_ref[...], hnt_ref[...])

    def bcast_rows(ref, h, ii, ls):
        return jnp.broadcast_to(ref[h, ii:ii + 1, ls], (BF16_ROWS, LANES)).astype(BF16)

    zero = jnp.zeros((BF16_ROWS, LANES), BF16)
    for ii in range(ib):
        for s in range(tt // LANES):
            ls = slice(s * LANES, (s + 1) * LANES)
            n_rows = [bcast_rows(n1_ref, h, ii, ls) for h in range(PEER_HEADS)]
            a_rows = [bcast_rows(a1_ref, h, ii, ls) for h in range(PEER_HEADS)]
            for jt in range(n_keys // BF16_ROWS):
                kr = slice(jt * BF16_ROWS, (jt + 1) * BF16_ROWS)
                gate = None
                for h in range(PEER_HEADS):
                    term = jnp.where(tab_ref[2 * h, kr, ls] < n_rows[h],
                                     tab_ref[2 * h + 1, kr, ls], zero)
                    term = term * a_rows[h]
                    gate = term if gate is None else gate + term
                er = slice(ii * n_keys + jt * BF16_ROWS, ii * n_keys + (jt + 1) * BF16_ROWS)
                pre = s_ref[er, ls]
                act = 0.5 * pre * (1.0 + lax.erf(pre * sqrt_half))
                hid_ref[er, ls] = act.astype(BF16) * gate

    acc_ref[:, :tt] += _dot(vt_ref[...], hid_ref[...])

    @pl.when(step == pl.num_programs(1) - 1)
    def _():
        total = h_ref[...] + acc_ref[:, :tt].T
        out_ref[...] = _rmsnorm_rows(total, lnf_ref[...])


def _peer_call(hnt, r2, e2, n1, a1, u_bf, vt_bf, h, lnf, tt, ib):
    d, t = hnt.shape
    nh, n_keys, _ = r2.shape
    n_exp = u_bf.shape[0]
    eb = ib * n_keys
    full = pl.BlockSpec((nh, n_keys, tt), lambda i, j: (0, 0, i))
    rows = pl.BlockSpec((nh, ib, tt), lambda i, j: (0, j, i))
    return pl.pallas_call(
        functools.partial(_peer_kernel, ib),
        grid=(t // tt, n_exp // eb),
        in_specs=[
            pl.BlockSpec((d, tt), lambda i, j: (0, i)),
            full, full, rows, rows,
            pl.BlockSpec((eb, d), lambda i, j: (j, 0)),
            pl.BlockSpec((d, eb), lambda i, j: (0, j)),
            pl.BlockSpec((tt, d), lambda i, j: (i, 0)),
            pl.BlockSpec((1, d), lambda i, j: (0, 0)),
        ],
        out_specs=pl.BlockSpec((tt, d), lambda i, j: (i, 0)),
        out_shape=jax.ShapeDtypeStruct((t, d), F32),
        scratch_shapes=[
            pltpu.VMEM((d, tt + LANES), F32),
            pltpu.VMEM((eb, tt + LANES), F32),
            pltpu.VMEM((eb, tt), BF16),
            pltpu.VMEM((2 * nh, n_keys + BF16_ROWS, tt + LANES), BF16),
        ],
        compiler_params=_params(("parallel", "arbitrary")),
        name="peer_experts",
    )(hnt, r2, e2, n1, a1, u_bf, vt_bf, h, lnf)


def _layer(h2, bsz, lb, ln_mix_w, w_in, conv_w, conv_b, dt_bias, a_log, d_skip, ssd_norm_w,
           hg_norm_w, w_out, ln_ffn_w, w_query, sub_keys, u_table, v_table, ln_out_w,
           tm, lc_ssd, lc_hg, sb_hg, tk, tt, ib):
    t, d = h2.shape
    inner = SSD_HEADS * SSD_HEAD_DIM
    conv_ch = inner + 2 * SSD_GROUPS * SSD_STATE
    hgk = HG_HEADS * HG_DIM

    sizes = (inner, conv_ch, SSD_HEADS, hgk, hgk, hgk, hgk)
    offs = np.concatenate([[0], np.cumsum(sizes)])
    wz, wxbc, wdt, wq_, wf, wi, wg = [w_in[:, offs[n]:offs[n + 1]] for n in range(7)]
    wdt_pad = jnp.pad(wdt, ((0, 0), (0, LANES - SSD_HEADS)))
    w_pad = jnp.concatenate([wz, wxbc, wq_, wf, wi, wg, wdt_pad], axis=1).astype(BF16)
    widths = (inner, conv_ch, hgk, hgk, hgk, hgk, LANES)
    starts = np.concatenate([[0], np.cumsum(widths)])[:-1]
    col_slices = tuple((int(s), int(w)) for s, w in zip(starts, widths))

    z, xbc, q_raw, f_raw, i_raw, g_raw, dt_raw = _inproj_call(
        h2, ln_mix_w.reshape(1, d), w_pad, col_slices, tm)

    pad_h = (0, LANES - SSD_HEADS)
    dtb = jnp.pad(dt_bias.astype(F32), pad_h).reshape(1, LANES)
    ahead = jnp.pad(-jnp.exp(a_log.astype(F32)), pad_h).reshape(1, LANES)
    dskip_x = jnp.repeat(d_skip.astype(F32), SSD_HEAD_DIM).reshape(1, inner)
    expand = (jnp.arange(LANES)[:, None] == (jnp.arange(inner)[None, :] // SSD_HEAD_DIM)).astype(BF16)
    y_ssd = _ssd_call(z, xbc, dt_raw, conv_w.astype(F32), conv_b.reshape(1, conv_ch).astype(F32),
                      dtb, ahead, dskip_x, ssd_norm_w.reshape(1, inner), expand, bsz, lc_ssd)

    y_hg = _hgrn2_call(q_raw, f_raw, i_raw, g_raw, lb.reshape(1, hgk), hg_norm_w.reshape(1, hgk),
                       bsz, lc_hg, sb_hg)

    n_hp = PEER_HEADS * 2
    half = sub_keys.shape[-1]
    h_res, hn_t, st = _outproj_call(h2, y_ssd, y_hg, w_out.astype(BF16), ln_ffn_w.reshape(1, d),
                                    w_query.T.astype(BF16),
                                    sub_keys.reshape(n_hp, PEER_KEYS, half).astype(BF16), tm)

    r2, e2, n1, a1 = _topk_call(st, tk)

    return _peer_call(hn_t, r2, e2, n1, a1, u_table.astype(BF16), v_table.T.astype(BF16),
                      h_res, ln_out_w.reshape(1, d), tt, ib)


def kernel(x, ln_mix_w, w_in, conv_w, conv_b, dt_bias, a_log, d_skip, ssd_norm_w, lb_logits,
           hg_norm_w, w_out, ln_ffn_w, w_query, sub_keys, u_table, v_table, ln_final_w):
    bsz, seqlen, d = x.shape
    depth = w_in.shape[0]
    assert depth == 1, "the final RMSNorm is fused into the (single) layer's expert kernel"
    lb_all = jnp.cumsum(jax.nn.softmax(lb_logits.astype(F32), axis=0), axis=0)
    t = bsz * seqlen
    tm = min(256, t)
    lc = min(128, seqlen)
    out = _layer(x.reshape(t, d), bsz, lb_all[0], ln_mix_w[0], w_in[0], conv_w[0], conv_b[0],
                 dt_bias[0], a_log[0], d_skip[0], ssd_norm_w[0], hg_norm_w[0], w_out[0],
                 ln_ffn_w[0], w_query[0], sub_keys[0], u_table[0], v_table[0], ln_final_w,
                 tm=tm, lc_ssd=lc, lc_hg=lc, sb_hg=32, tk=min(512, t),
                 tt=min(512, t), ib=SUBLANES)
    return out.reshape(bsz, seqlen, d)
```

```python
import functools

import numpy as np
import jax
import jax.numpy as jnp
from jax import lax
from jax.experimental import pallas as pl
from jax.experimental.pallas import tpu as pltpu

F32 = jnp.float32
BF16 = jnp.bfloat16
EPS = 1e-6

SSD_HEADS = 16
SSD_HEAD_DIM = 64
SSD_GROUPS = 4
SSD_STATE = 128
SSD_CONV = 4
HG_HEADS = 8
HG_DIM = 128
PEER_HEADS = 8
PEER_KEYS = 128
PEER_TOPK = 16

LANES = 128
SUBLANES = 8
VMEM_LIMIT = 56 * 1024 * 1024

CONV_PAD = SUBLANES


def _params(sem):
    return pltpu.CompilerParams(dimension_semantics=sem, vmem_limit_bytes=VMEM_LIMIT)


def _split3(x):
    hi = x.astype(BF16)
    r1 = x - hi.astype(F32)
    mid = r1.astype(BF16)
    lo = (r1 - mid.astype(F32)).astype(BF16)
    return hi, mid, lo


def _split2(x):
    hi = x.astype(BF16)
    lo = (x - hi.astype(F32)).astype(BF16)
    return hi, lo


def _dot(a, b):
    return jnp.dot(a, b, preferred_element_type=F32)


def _dot_nt(a, b):
    return lax.dot_general(a, b, (((1,), (1,)), ((), ())), preferred_element_type=F32)


def _dot_exact_rhs(m01, x):
    hi, mid, lo = _split3(x)
    return _dot(m01, hi) + _dot(m01, mid) + _dot(m01, lo)


def _dot_exact_lhs(x, m01):
    hi, mid, lo = _split3(x)
    return _dot(hi, m01) + _dot(mid, m01) + _dot(lo, m01)


def _silu(x):
    return x * jax.nn.sigmoid(x)


def _softplus(x):
    return jnp.maximum(x, 0.0) + jnp.log1p(jnp.exp(-jnp.abs(x)))


def _rmsnorm_rows(x, w):
    ms = jnp.mean(x * x, axis=-1, keepdims=True)
    return x * lax.rsqrt(ms + EPS) * w


def _inproj_kernel(col_slices, x_ref, lnw_ref, w_ref, *out_refs):
    hn = _rmsnorm_rows(x_ref[...], lnw_ref[...]).astype(BF16)
    for ref, (off, width) in zip(out_refs, col_slices):
        ref[...] = _dot(hn, w_ref[:, off:off + width]).astype(ref.dtype)


def _inproj_call(x2, lnw, w_pad, col_slices, tm):
    t, d = x2.shape
    out_shape = [jax.ShapeDtypeStruct((t, width), F32) for _, width in col_slices]
    out_specs = [pl.BlockSpec((tm, width), lambda i: (i, 0)) for _, width in col_slices]
    return pl.pallas_call(
        functools.partial(_inproj_kernel, col_slices),
        grid=(t // tm,),
        in_specs=[
            pl.BlockSpec((tm, d), lambda i: (i, 0)),
            pl.BlockSpec((1, d), lambda i: (0, 0)),
            pl.BlockSpec(w_pad.shape, lambda i: (0, 0), pipeline_mode=pl.Buffered(1)),
        ],
        out_specs=out_specs,
        out_shape=out_shape,
        compiler_params=_params(("parallel",)),
        name="inproj",
    )(x2, lnw, w_pad)


def _ssd_kernel(lc, z_ref, xbc_ref, dt_ref, convw_ref, convb_ref, dtb_ref, ahead_ref, dskip_ref,
                normw_ref, expand_ref, y_ref, xpad_ref, state_ref, ybuf_ref):
    inner = SSD_HEADS * SSD_HEAD_DIM
    gn = SSD_GROUPS * SSD_STATE
    pair_w = 2 * SSD_HEAD_DIM

    @pl.when(pl.program_id(1) == 0)
    def _():
        xpad_ref[0:CONV_PAD, :] = jnp.zeros((CONV_PAD, xpad_ref.shape[1]), F32)
        state_ref[...] = jnp.zeros(state_ref.shape, F32)

    xpad_ref[CONV_PAD:CONV_PAD + lc, :] = xbc_ref[...]
    conv = convb_ref[...]
    for k in range(SSD_CONV):
        start = CONV_PAD - (SSD_CONV - 1) + k
        conv = conv + convw_ref[k:k + 1, :] * xpad_ref[start:start + lc, :]
    xpad_ref[0:CONV_PAD, :] = xpad_ref[lc:lc + CONV_PAD, :]
    xbc = _silu(conv)
    xs = xbc[:, :inner]

    dt = _softplus(dt_ref[...] + dtb_ref[...])
    a = dt * ahead_ref[...]

    row = lax.broadcasted_iota(jnp.int32, (lc, lc), 0)
    col = lax.broadcasted_iota(jnp.int32, (lc, lc), 1)
    tril = row >= col
    tri_bf = tril.astype(BF16)
    acum = _dot_exact_rhs(tri_bf, a)
    acum_t = acum.T
    a_last = acum[lc - 1:lc, :]

    expand = expand_ref[...]
    dt_x = _dot_exact_lhs(dt, expand)
    from_start_x = _dot_exact_lhs(jnp.exp(acum), expand)
    to_end_x = _dot_exact_lhs(jnp.exp(a_last - acum), expand)
    chunk_decay = jnp.exp(a_last)
    xdt = xs * dt_x
    xw = xdt * to_end_x

    lane = lax.broadcasted_iota(jnp.int32, (lc, pair_w), 1)
    first_half = lane < SSD_HEAD_DIM

    heads_per_group = SSD_HEADS // SSD_GROUPS
    for g in range(SSD_GROUPS):
        bm = xbc[:, inner + g * SSD_STATE: inner + (g + 1) * SSD_STATE].astype(BF16)
        cm = xbc[:, inner + gn + g * SSD_STATE: inner + gn + (g + 1) * SSD_STATE].astype(BF16)
        cb = _dot_nt(cm, bm)
        for pr in range(heads_per_group // 2):
            pidx = g * (heads_per_group // 2) + pr
            lo, hi = pidx * pair_w, (pidx + 1) * pair_w
            xdt_p = xdt[:, lo:hi]
            y_pair = None
            for sub in range(2):
                h = 2 * pidx + sub
                diff = acum[:, h:h + 1] - acum_t[h:h + 1, :]
                decay = jnp.where(tril, jnp.exp(jnp.where(tril, diff, 0.0)), 0.0)
                m_h = (cb * decay).astype(BF16)
                sel = first_half if sub == 0 else jnp.logical_not(first_half)
                contrib = _dot(m_h, jnp.where(sel, xdt_p, 0.0).astype(BF16))
                y_pair = contrib if y_pair is None else y_pair + contrib
            st = state_ref[lo:hi, :]
            y_off = _dot_nt(cm, st.astype(BF16)) * from_start_x[:, lo:hi]
            ybuf_ref[:, lo:hi] = y_pair + y_off + xs[:, lo:hi] * dskip_ref[:, lo:hi]
            r_idx = lax.broadcasted_iota(jnp.int32, (pair_w, SSD_STATE), 0)
            dec = jnp.where(r_idx < SSD_HEAD_DIM, chunk_decay[:, 2 * pidx:2 * pidx + 1],
                            chunk_decay[:, 2 * pidx + 1:2 * pidx + 2])
            state_ref[lo:hi, :] = st * dec + _dot(xw[:, lo:hi].T.astype(BF16), bm)

    y = ybuf_ref[...] * _silu(z_ref[...])
    y_ref[...] = _rmsnorm_rows(y, normw_ref[...]).astype(y_ref.dtype)


def _ssd_call(z, xbc, dtp, convw, convb, dtb, ahead, dskip_x, normw, expand, bsz, lc):
    t, inner = z.shape
    cw = xbc.shape[1]
    nc = t // bsz // lc
    tok = lambda b, c: (b * nc + c, 0)
    const = lambda b, c: (0, 0)
    return pl.pallas_call(
        functools.partial(_ssd_kernel, lc),
        grid=(bsz, nc),
        in_specs=[
            pl.BlockSpec((lc, inner), tok),
            pl.BlockSpec((lc, cw), tok),
            pl.BlockSpec((lc, LANES), tok),
            pl.BlockSpec(convw.shape, const),
            pl.BlockSpec(convb.shape, const),
            pl.BlockSpec(dtb.shape, const),
            pl.BlockSpec(ahead.shape, const),
            pl.BlockSpec(dskip_x.shape, const),
            pl.BlockSpec(normw.shape, const),
            pl.BlockSpec(expand.shape, const),
        ],
        out_specs=pl.BlockSpec((lc, inner), tok),
        out_shape=jax.ShapeDtypeStruct((t, inner), BF16),
        scratch_shapes=[
            pltpu.VMEM((lc + CONV_PAD, cw), F32),
            pltpu.VMEM((inner, SSD_STATE), F32),
            pltpu.VMEM((lc, inner), F32),
        ],
        compiler_params=_params(("parallel", "arbitrary")),
        name="ssd",
    )(z, xbc, dtp, convw, convb, dtb, ahead, dskip_x, normw, expand)


def _hgrn2_kernel(lc, sb, q_ref, f_ref, i_ref, g_ref, lb_ref, normw_ref, y_ref, state_ref):
    @pl.when(pl.program_id(0) == 0)
    def _():
        state_ref[...] = jnp.zeros(state_ref.shape, F32)

    for b in range(q_ref.shape[0]):
        _hgrn2_chunk(lc, sb, q_ref.at[b], f_ref.at[b], i_ref.at[b], g_ref.at[b], lb_ref, normw_ref,
                     y_ref.at[b], state_ref.at[b])


def _hgrn2_chunk(lc, sb, q_ref, f_ref, i_ref, g_ref, lb_ref, normw_ref, y_ref, state_ref):
    nsb = lc // sb
    lb = lb_ref[...]
    f = lb + (1.0 - lb) * jax.nn.sigmoid(f_ref[...])
    logf = jnp.log(f)
    kk = 1.0 - f
    q = _silu(q_ref[...])
    v = i_ref[...]
    v_bf = v.astype(BF16)

    row = lax.broadcasted_iota(jnp.int32, (lc, lc), 0)
    col = lax.broadcasted_iota(jnp.int32, (lc, lc), 1)
    tri_bf = (row >= col).astype(BF16)
    gcum = _dot_exact_rhs(tri_bf, logf)
    g_end = gcum[lc - 1:lc, :]
    k_end_bf = (kk * jnp.exp(g_end - gcum)).astype(BF16)
    state_decay = jnp.exp(g_end)

    q_inter_bf = (q * jnp.exp(gcum)).astype(BF16)
    key_row = lax.broadcasted_iota(jnp.int32, kk.shape, 0)
    qs_bf, k_ref_bf = [], []
    for ib in range(nsb):
        r0, r1 = ib * sb, (ib + 1) * sb
        if ib == 0:
            gref = jnp.zeros_like(g_end)
        else:
            gref = gcum[r0 - 1:r0, :]
        qs_bf.append((q[r0:r1, :] * jnp.exp(gcum[r0:r1, :] - gref)).astype(BF16))
        k_ref = kk * jnp.exp(gref - gcum)
        if r1 < lc:
            k_ref = jnp.where(key_row < r1, k_ref, 0.0)
        k_ref_bf.append(k_ref.astype(BF16))

    causal = row >= col
    heads = [(h * HG_DIM, (h + 1) * HG_DIM) for h in range(HG_HEADS)]
    states = [state_ref[lo:hi, :] for lo, hi in heads]
    o_inter = [_dot_nt(q_inter_bf[:, lo:hi], st.astype(BF16)) for (lo, hi), st in zip(heads, states)]
    for (lo, hi), st in zip(heads, states):
        state_ref[lo:hi, :] = (st * state_decay[:, lo:hi]
                               + _dot(v[:, lo:hi].T.astype(BF16), k_end_bf[:, lo:hi]))
    atts = [jnp.concatenate([_dot_nt(qs_bf[ib][:, lo:hi], k_ref_bf[ib][:, lo:hi])
                             for ib in range(nsb)], axis=0) for lo, hi in heads]
    atts = [jnp.where(causal, att, 0.0).astype(BF16) for att in atts]
    outs = [_dot(att, v_bf[:, lo:hi]) + oi for att, oi, (lo, hi) in zip(atts, o_inter, heads)]
    for o, (lo, hi) in zip(outs, heads):
        gate = _silu(g_ref[:, lo:hi])
        y_ref[:, lo:hi] = (_rmsnorm_rows(o, normw_ref[:, lo:hi]) * gate).astype(y_ref.dtype)


def _hgrn2_call(q, f, i, g, lb, normw, bsz, lc, sb):
    t, kd = q.shape
    seq = t // bsz
    tok = pl.BlockSpec((bsz, lc, kd), lambda c: (0, c, 0))
    const = lambda c: (0, 0)
    by_row = lambda a: a.reshape(bsz, seq, kd)
    y = pl.pallas_call(
        functools.partial(_hgrn2_kernel, lc, sb),
        grid=(seq // lc,),
        in_specs=[tok, tok, tok, tok, pl.BlockSpec(lb.shape, const), pl.BlockSpec(normw.shape, const)],
        out_specs=tok,
        out_shape=jax.ShapeDtypeStruct((bsz, seq, kd), BF16),
        scratch_shapes=[pltpu.VMEM((bsz, HG_HEADS * HG_DIM, HG_DIM), F32)],
        compiler_params=_params(("arbitrary",)),
        name="hgrn2",
    )(by_row(q), by_row(f), by_row(i), by_row(g), lb, normw)
    return y.reshape(t, kd)


def _outproj_kernel(x_ref, yssd_ref, yhg_ref, wout_ref, lnw_ref, wq_ref, keys_ref,
                    h_ref, hnt_ref, st_ref):
    inner = yssd_ref.shape[1]
    h = x_ref[...] + _dot(yssd_ref[...], wout_ref[:inner, :]) + _dot(yhg_ref[...], wout_ref[inner:, :])
    h_ref[...] = h
    hn_t = _rmsnorm_rows(h, lnw_ref[...]).T.astype(BF16)
    hnt_ref[...] = hn_t
    q_t = _dot(wq_ref[...], hn_t)
    n_hp = keys_ref.shape[0]
    half = keys_ref.shape[2]
    for hp in range(n_hp):
        st_ref[hp] = _dot(keys_ref[hp], q_t[hp * half:(hp + 1) * half, :].astype(BF16))


def _outproj_call(x2, y_ssd, y_hg, w_out, lnw, wq, keys, tm):
    t, d = x2.shape
    n_hp, n_keys, half = keys.shape
    const2 = lambda i: (0, 0)
    const3 = lambda i: (0, 0, 0)
    one = pl.Buffered(1)
    return pl.pallas_call(
        _outproj_kernel,
        grid=(t // tm,),
        in_specs=[
            pl.BlockSpec((tm, d), lambda i: (i, 0)),
            pl.BlockSpec((tm, y_ssd.shape[1]), lambda i: (i, 0)),
            pl.BlockSpec((tm, y_hg.shape[1]), lambda i: (i, 0)),
            pl.BlockSpec(w_out.shape, const2, pipeline_mode=one),
            pl.BlockSpec(lnw.shape, const2),
            pl.BlockSpec(wq.shape, const2, pipeline_mode=one),
            pl.BlockSpec(keys.shape, const3, pipeline_mode=one),
        ],
        out_specs=[
            pl.BlockSpec((tm, d), lambda i: (i, 0)),
            pl.BlockSpec((d, tm), lambda i: (0, i)),
            pl.BlockSpec((n_hp, n_keys, tm), lambda i: (0, 0, i)),
        ],
        out_shape=[
            jax.ShapeDtypeStruct((t, d), F32),
            jax.ShapeDtypeStruct((d, t), BF16),
            jax.ShapeDtypeStruct((n_hp, n_keys, t), F32),
        ],
        compiler_params=_params(("parallel",)),
        name="outproj",
    )(x2, y_ssd, y_hg, w_out, lnw, wq, keys)


NO_RANK = 255.0


def _extract_top(v, n_take, out_ref, want_rank):
    rank = jnp.full(v.shape, NO_RANK, F32) if want_rank else None
    for r in range(n_take):
        m = jnp.max(v, axis=0, keepdims=True)
        out_ref[r:r + 1, :] = m
        if want_rank or r + 1 < n_take:
            hit = v == m
            if want_rank:
                rank = jnp.where(hit, np.float32(r), rank)
            if r + 1 < n_take:
                v = jnp.where(hit, -jnp.inf, v)
    return rank


def _topk_kernel(st_ref, r2_ref, e2_ref, n1_ref, a1_ref, *list_refs):
    n_keys = st_ref.shape[1]
    tk = st_ref.shape[2]
    k = PEER_TOPK
    k_iota = lax.broadcasted_iota(jnp.int32, (k, LANES), 0)

    def one_head(ls, h, t1_ref, t2_ref, tc_ref):
        s1 = st_ref[2 * h, :, ls]
        s2 = st_ref[2 * h + 1, :, ls]
        _extract_top(s1, k, t1_ref, False)
        rank2 = _extract_top(s2, k, t2_ref, True)
        t1 = t1_ref[...]
        t2 = t2_ref[...]
        tiles = [t1[0:1, :] + t2]
        for a in range(1, k // 2):
            nb = k // (a + 1)
            tiles.append(jnp.where(k_iota[:SUBLANES] < nb, t1[a:a + 1, :] + t2[:SUBLANES, :],
                                   -jnp.inf))
        tiles.append(t1[k // 2:, :] + t2[0:1, :])
        cand = jnp.concatenate(tiles, axis=0)
        _extract_top(cand, k, tc_ref, False)
        top = tc_ref[...]
        cmax = top[0:1, :]
        z = jnp.sum(jnp.exp(top - cmax), axis=0, keepdims=True)
        tau = top[k - 1:k, :]
        passed = jnp.where(jnp.logical_and(cand >= tau, cand > -jnp.inf), 1.0, 0.0)
        counts = [jnp.sum(passed[0:k, :], axis=0, keepdims=True)]
        for a in range(1, k // 2):
            r0 = k + (a - 1) * SUBLANES
            counts.append(jnp.sum(passed[r0:r0 + SUBLANES, :], axis=0, keepdims=True))
        r0 = k + (k // 2 - 1) * SUBLANES
        counts += [passed[r0 + a:r0 + a + 1, :] for a in range(k - k // 2)]
        n_row = jnp.zeros((n_keys, LANES), F32)
        for a in range(k):
            n_row = jnp.where(s1 == t1[a:a + 1, :], counts[a], n_row)
        r2_ref[h, :, ls] = rank2
        e2_ref[h, :, ls] = jnp.exp(s2 - t2[0:1, :])
        n1_ref[h, :, ls] = n_row
        a1_ref[h, :, ls] = jnp.exp(s1 - t1[0:1, :]) / z

    def strip(s, carry):
        ls = pl.ds(pl.multiple_of(s * LANES, LANES), LANES)
        for h in range(PEER_HEADS):
            one_head(ls, h, *list_refs[3 * h:3 * h + 3])
        return carry

    lax.fori_loop(0, tk // LANES, strip, 0)


def _topk_call(st, tk):
    n_hp, n_keys, t = st.shape
    nh = n_hp // 2
    blk = pl.BlockSpec((nh, n_keys, tk), lambda i: (0, 0, i))
    per_key = jax.ShapeDtypeStruct((nh, n_keys, t), F32)
    return pl.pallas_call(
        _topk_kernel,
        grid=(t // tk,),
        in_specs=[pl.BlockSpec((n_hp, n_keys, tk), lambda i: (0, 0, i))],
        out_specs=[blk, blk, blk, blk],
        out_shape=[per_key, per_key, per_key, per_key],
        scratch_shapes=[pltpu.VMEM((PEER_TOPK, LANES), F32)] * (3 * PEER_HEADS),
        compiler_params=_params(("parallel",)),
        name="peer_topk",
    )(st)


BF16_ROWS = 2 * SUBLANES


def _peer_kernel(ib, hnt_ref, r2_ref, e2_ref, n1_ref, a1_ref, u_ref, vt_ref, h_ref, lnf_ref,
                 out_ref, acc_ref, s_ref, hid_ref, tab_ref):
    step = pl.program_id(1)
    n_keys = r2_ref.shape[1]
    tt = hnt_ref.shape[1]
    sqrt_half = np.float32(np.sqrt(0.5))

    @pl.when(step == 0)
    def _():
        acc_ref[...] = jnp.zeros(acc_ref.shape, F32)
        for h in range(PEER_HEADS):
            tab_ref[2 * h, :n_keys, :tt] = r2_ref[h].astype(BF16)
            tab_ref[2 * h + 1, :n_keys, :tt] = e2_ref[h].astype(BF16)

    s_ref[:, :tt] = _dot(u_ref[...], hnt_ref[...])

    def bcast_rows(ref, h, ii, ls):
        return jnp.broadcast_to(ref[h, ii:ii + 1, ls], (BF16_ROWS, LANES)).astype(BF16)

    zero = jnp.zeros((BF16_ROWS, LANES), BF16)
    for ii in range(ib):
        for s in range(tt // LANES):
            ls = slice(s * LANES, (s + 1) * LANES)
            n_rows = [bcast_rows(n1_ref, h, ii, ls) for h in range(PEER_HEADS)]
            a_rows = [bcast_rows(a1_ref, h, ii, ls) for h in range(PEER_HEADS)]
            for jt in range(n_keys // BF16_ROWS):
                kr = slice(jt * BF16_ROWS, (jt + 1) * BF16_ROWS)
                gate = None
                for h in range(PEER_HEADS):
                    term = jnp.where(tab_ref[2 * h, kr, ls] < n_rows[h],
                                     tab_ref[2 * h + 1, kr, ls], zero)
                    term = term * a_rows[h]
                    gate = term if gate is None else gate + term
                er = slice(ii * n_keys + jt * BF16_ROWS, ii * n_keys + (jt + 1) * BF16_ROWS)
                pre = s_ref[er, ls]
                act = 0.5 * pre * (1.0 + lax.erf(pre * sqrt_half))
                hid_ref[er, ls] = act.astype(BF16) * gate

    acc_ref[:, :tt] += _dot(vt_ref[...], hid_ref[...])

    @pl.when(step == pl.num_programs(1) - 1)
    def _():
        total = h_ref[...] + acc_ref[:, :tt].T
        out_ref[...] = _rmsnorm_rows(total, lnf_ref[...])


def _peer_call(hnt, r2, e2, n1, a1, u_bf, vt_bf, h, lnf, tt, ib):
    d, t = hnt.shape
    nh, n_keys, _ = r2.shape
    n_exp = u_bf.shape[0]
    eb = ib * n_keys
    full = pl.BlockSpec((nh, n_keys, tt), lambda i, j: (0, 0, i))
    rows = pl.BlockSpec((nh, ib, tt), lambda i, j: (0, j, i))
    return pl.pallas_call(
        functools.partial(_peer_kernel, ib),
        grid=(t // tt, n_exp // eb),
        in_specs=[
            pl.BlockSpec((d, tt), lambda i, j: (0, i)),
            full, full, rows, rows,
            pl.BlockSpec((eb, d), lambda i, j: (j, 0)),
            pl.BlockSpec((d, eb), lambda i, j: (0, j)),
            pl.BlockSpec((tt, d), lambda i, j: (i, 0)),
            pl.BlockSpec((1, d), lambda i, j: (0, 0)),
        ],
        out_specs=pl.BlockSpec((tt, d), lambda i, j: (i, 0)),
        out_shape=jax.ShapeDtypeStruct((t, d), F32),
        scratch_shapes=[
            pltpu.VMEM((d, tt + LANES), F32),
            pltpu.VMEM((eb, tt + LANES), F32),
            pltpu.VMEM((eb, tt), BF16),
            pltpu.VMEM((2 * nh, n_keys + BF16_ROWS, tt + LANES), BF16),
        ],
        compiler_params=_params(("parallel", "arbitrary")),
        name="peer_experts",
    )(hnt, r2, e2, n1, a1, u_bf, vt_bf, h, lnf)


def _layer(h2, bsz, lb, ln_mix_w, w_in, conv_w, conv_b, dt_bias, a_log, d_skip, ssd_norm_w,
           hg_norm_w, w_out, ln_ffn_w, w_query, sub_keys, u_table, v_table, ln_out_w,
           tm, lc_ssd, lc_hg, sb_hg, tk, tt, ib):
    t, d = h2.shape
    inner = SSD_HEADS * SSD_HEAD_DIM
    conv_ch = inner + 2 * SSD_GROUPS * SSD_STATE
    hgk = HG_HEADS * HG_DIM

    sizes = (inner, conv_ch, SSD_HEADS, hgk, hgk, hgk, hgk)
    offs = np.concatenate([[0], np.cumsum(sizes)])
    wz, wxbc, wdt, wq_, wf, wi, wg = [w_in[:, offs[n]:offs[n + 1]] for n in range(7)]
    wdt_pad = jnp.pad(wdt, ((0, 0), (0, LANES - SSD_HEADS)))
    w_pad = jnp.concatenate([wz, wxbc, wq_, wf, wi, wg, wdt_pad], axis=1).astype(BF16)
    widths = (inner, conv_ch, hgk, hgk, hgk, hgk, LANES)
    starts = np.concatenate([[0], np.cumsum(widths)])[:-1]
    col_slices = tuple((int(s), int(w)) for s, w in zip(starts, widths))

    z, xbc, q_raw, f_raw, i_raw, g_raw, dt_raw = _inproj_call(
        h2, ln_mix_w.reshape(1, d), w_pad, col_slices, tm)

    pad_h = (0, LANES - SSD_HEADS)
    dtb = jnp.pad(dt_bias.astype(F32), pad_h).reshape(1, LANES)
    ahead = jnp.pad(-jnp.exp(a_log.astype(F32)), pad_h).reshape(1, LANES)
    dskip_x = jnp.repeat(d_skip.astype(F32), SSD_HEAD_DIM).reshape(1, inner)
    expand = (jnp.arange(LANES)[:, None] == (jnp.arange(inner)[None, :] // SSD_HEAD_DIM)).astype(BF16)
    y_ssd = _ssd_call(z, xbc, dt_raw, conv_w.astype(F32), conv_b.reshape(1, conv_ch).astype(F32),
                      dtb, ahead, dskip_x, ssd_norm_w.reshape(1, inner), expand, bsz, lc_ssd)

    y_hg = _hgrn2_call(q_raw, f_raw, i_raw, g_raw, lb.reshape(1, hgk), hg_norm_w.reshape(1, hgk),
                       bsz, lc_hg, sb_hg)

    n_hp = PEER_HEADS * 2
    half = sub_keys.shape[-1]
    h_res, hn_t, st = _outproj_call(h2, y_ssd, y_hg, w_out.astype(BF16), ln_ffn_w.reshape(1, d),
                                    w_query.T.astype(BF16),
                                    sub_keys.reshape(n_hp, PEER_KEYS, half).astype(BF16), tm)

    r2, e2, n1, a1 = _topk_call(st, tk)

    return _peer_call(hn_t, r2, e2, n1, a1, u_table.astype(BF16), v_table.T.astype(BF16),
                      h_res, ln_out_w.reshape(1, d), tt, ib)


def kernel(x, ln_mix_w, w_in, conv_w, conv_b, dt_bias, a_log, d_skip, ssd_norm_w, lb_logits,
           hg_norm_w, w_out, ln_ffn_w, w_query, sub_keys, u_table, v_table, ln_final_w):
    bsz, seqlen, d = x.shape
    depth = w_in.shape[0]
    assert depth == 1, "the final RMSNorm is fused into the (single) layer's expert kernel"
    lb_all = jnp.cumsum(jax.nn.softmax(lb_logits.astype(F32), axis=0), axis=0)
    t = bsz * seqlen
    tm = min(256, t)
    lc = min(128, seqlen)
    out = _layer(x.reshape(t, d), bsz, lb_all[0], ln_mix_w[0], w_in[0], conv_w[0], conv_b[0],
                 dt_bias[0], a_log[0], d_skip[0], ssd_norm_w[0], hg_norm_w[0], w_out[0],
                 ln_ffn_w[0], w_query[0], sub_keys[0], u_table[0], v_table[0], ln_final_w,
                 tm=tm, lc_ssd=lc, lc_hg=lc, sb_hg=32, tk=min(512, t),
                 tt=min(512, t), ib=SUBLANES)
    return out.reshape(bsz, seqlen, d)
```

```python
import functools

import numpy as np
import jax
import jax.numpy as jnp
from jax import lax
from jax.experimental import pallas as pl
from jax.experimental.pallas import tpu as pltpu

F32 = jnp.float32
BF16 = jnp.bfloat16
EPS = 1e-6

SSD_HEADS = 16
SSD_HEAD_DIM = 64
SSD_GROUPS = 4
SSD_STATE = 128
SSD_CONV = 4
HG_HEADS = 8
HG_DIM = 128
PEER_HEADS = 8
PEER_KEYS = 128
PEER_TOPK = 16

LANES = 128
SUBLANES = 8
VMEM_LIMIT = 56 * 1024 * 1024

CONV_PAD = SUBLANES


def _params(sem):
    return pltpu.CompilerParams(dimension_semantics=sem, vmem_limit_bytes=VMEM_LIMIT)


def _split3(x):
    hi = x.astype(BF16)
    r1 = x - hi.astype(F32)
    mid = r1.astype(BF16)
    lo = (r1 - mid.astype(F32)).astype(BF16)
    return hi, mid, lo


def _split2(x):
    hi = x.astype(BF16)
    lo = (x - hi.astype(F32)).astype(BF16)
    return hi, lo


def _dot(a, b):
    return jnp.dot(a, b, preferred_element_type=F32)


def _dot_nt(a, b):
    return lax.dot_general(a, b, (((1,), (1,)), ((), ())), preferred_element_type=F32)


def _dot_exact_rhs(m01, x):
    hi, mid, lo = _split3(x)
    return _dot(m01, hi) + _dot(m01, mid) + _dot(m01, lo)


def _dot_exact_lhs(x, m01):
    hi, mid, lo = _split3(x)
    return _dot(hi, m01) + _dot(mid, m01) + _dot(lo, m01)


def _silu(x):
    return x * jax.nn.sigmoid(x)


def _softplus(x):
    return jnp.maximum(x, 0.0) + jnp.log1p(jnp.exp(-jnp.abs(x)))


def _rmsnorm_rows(x, w):
    ms = jnp.mean(x * x, axis=-1, keepdims=True)
    return x * lax.rsqrt(ms + EPS) * w


def _inproj_kernel(col_slices, x_ref, lnw_ref, w_ref, *out_refs):
    hn = _rmsnorm_rows(x_ref[...], lnw_ref[...]).astype(BF16)
    for ref, (off, width) in zip(out_refs, col_slices):
        ref[...] = _dot(hn, w_ref[:, off:off + width]).astype(ref.dtype)


def _inproj_call(x2, lnw, w_pad, col_slices, tm):
    t, d = x2.shape
    out_shape = [jax.ShapeDtypeStruct((t, width), F32) for _, width in col_slices]
    out_specs = [pl.BlockSpec((tm, width), lambda i: (i, 0)) for _, width in col_slices]
    return pl.pallas_call(
        functools.partial(_inproj_kernel, col_slices),
        grid=(t // tm,),
        in_specs=[
            pl.BlockSpec((tm, d), lambda i: (i, 0)),
            pl.BlockSpec((1, d), lambda i: (0, 0)),
            pl.BlockSpec(w_pad.shape, lambda i: (0, 0), pipeline_mode=pl.Buffered(1)),
        ],
        out_specs=out_specs,
        out_shape=out_shape,
        compiler_params=_params(("parallel",)),
        name="inproj",
    )(x2, lnw, w_pad)


def _ssd_kernel(lc, z_ref, xbc_ref, dt_ref, convw_ref, convb_ref, dtb_ref, ahead_ref, dskip_ref,
                normw_ref, expand_ref, y_ref, xpad_ref, state_ref, ybuf_ref):
    inner = SSD_HEADS * SSD_HEAD_DIM
    gn = SSD_GROUPS * SSD_STATE
    pair_w = 2 * SSD_HEAD_DIM

    @pl.when(pl.program_id(1) == 0)
    def _():
        xpad_ref[0:CONV_PAD, :] = jnp.zeros((CONV_PAD, xpad_ref.shape[1]), F32)
        state_ref[...] = jnp.zeros(state_ref.shape, F32)

    xpad_ref[CONV_PAD:CONV_PAD + lc, :] = xbc_ref[...]
    conv = convb_ref[...]
    for k in range(SSD_CONV):
        start = CONV_PAD - (SSD_CONV - 1) + k
        conv = conv + convw_ref[k:k + 1, :] * xpad_ref[start:start + lc, :]
    xpad_ref[0:CONV_PAD, :] = xpad_ref[lc:lc + CONV_PAD, :]
    xbc = _silu(conv)
    xs = xbc[:, :inner]

    dt = _softplus(dt_ref[...] + dtb_ref[...])
    a = dt * ahead_ref[...]

    row = lax.broadcasted_iota(jnp.int32, (lc, lc), 0)
    col = lax.broadcasted_iota(jnp.int32, (lc, lc), 1)
    tril = row >= col
    tri_bf = tril.astype(BF16)
    acum = _dot_exact_rhs(tri_bf, a)
    acum_t = acum.T
    a_last = acum[lc - 1:lc, :]

    expand = expand_ref[...]
    dt_x = _dot_exact_lhs(dt, expand)
    from_start_x = _dot_exact_lhs(jnp.exp(acum), expand)
    to_end_x = _dot_exact_lhs(jnp.exp(a_last - acum), expand)
    chunk_decay = jnp.exp(a_last)
    xdt = xs * dt_x
    xw = xdt * to_end_x

    lane = lax.broadcasted_iota(jnp.int32, (lc, pair_w), 1)
    first_half = lane < SSD_HEAD_DIM

    heads_per_group = SSD_HEADS // SSD_GROUPS
    for g in range(SSD_GROUPS):
        bm = xbc[:, inner + g * SSD_STATE: inner + (g + 1) * SSD_STATE].astype(BF16)
        cm = xbc[:, inner + gn + g * SSD_STATE: inner + gn + (g + 1) * SSD_STATE].astype(BF16)
        cb = _dot_nt(cm, bm)
        for pr in range(heads_per_group // 2):
            pidx = g * (heads_per_group // 2) + pr
            lo, hi = pidx * pair_w, (pidx + 1) * pair_w
            xdt_p = xdt[:, lo:hi]
            y_pair = None
            for sub in range(2):
                h = 2 * pidx + sub
                diff = acum[:, h:h + 1] - acum_t[h:h + 1, :]
                decay = jnp.where(tril, jnp.exp(jnp.where(tril, diff, 0.0)), 0.0)
                m_h = (cb * decay).astype(BF16)
                sel = first_half if sub == 0 else jnp.logical_not(first_half)
                contrib = _dot(m_h, jnp.where(sel, xdt_p, 0.0).astype(BF16))
                y_pair = contrib if y_pair is None else y_pair + contrib
            st = state_ref[lo:hi, :]
            y_off = _dot_nt(cm, st.astype(BF16)) * from_start_x[:, lo:hi]
            ybuf_ref[:, lo:hi] = y_pair + y_off + xs[:, lo:hi] * dskip_ref[:, lo:hi]
            r_idx = lax.broadcasted_iota(jnp.int32, (pair_w, SSD_STATE), 0)
            dec = jnp.where(r_idx < SSD_HEAD_DIM, chunk_decay[:, 2 * pidx:2 * pidx + 1],
                            chunk_decay[:, 2 * pidx + 1:2 * pidx + 2])
            state_ref[lo:hi, :] = st * dec + _dot(xw[:, lo:hi].T.astype(BF16), bm)

    y = ybuf_ref[...] * _silu(z_ref[...])
    y_ref[...] = _rmsnorm_rows(y, normw_ref[...]).astype(y_ref.dtype)


def _ssd_call(z, xbc, dtp, convw, convb, dtb, ahead, dskip_x, normw, expand, bsz, lc):
    t, inner = z.shape
    cw = xbc.shape[1]
    nc = t // bsz // lc
    tok = lambda b, c: (b * nc + c, 0)
    const = lambda b, c: (0, 0)
    return pl.pallas_call(
        functools.partial(_ssd_kernel, lc),
        grid=(bsz, nc),
        in_specs=[
            pl.BlockSpec((lc, inner), tok),
            pl.BlockSpec((lc, cw), tok),
            pl.BlockSpec((lc, LANES), tok),
            pl.BlockSpec(convw.shape, const),
            pl.BlockSpec(convb.shape, const),
            pl.BlockSpec(dtb.shape, const),
            pl.BlockSpec(ahead.shape, const),
            pl.BlockSpec(dskip_x.shape, const),
            pl.BlockSpec(normw.shape, const),
            pl.BlockSpec(expand.shape, const),
        ],
        out_specs=pl.BlockSpec((lc, inner), tok),
        out_shape=jax.ShapeDtypeStruct((t, inner), BF16),
        scratch_shapes=[
            pltpu.VMEM((lc + CONV_PAD, cw), F32),
            pltpu.VMEM((inner, SSD_STATE), F32),
            pltpu.VMEM((lc, inner), F32),
        ],
        compiler_params=_params(("parallel", "arbitrary")),
        name="ssd",
    )(z, xbc, dtp, convw, convb, dtb, ahead, dskip_x, normw, expand)


def _hgrn2_kernel(lc, sb, q_ref, f_ref, i_ref, g_ref, lb_ref, normw_ref, y_ref, state_ref):
    @pl.when(pl.program_id(0) == 0)
    def _():
        state_ref[...] = jnp.zeros(state_ref.shape, F32)

    for b in range(q_ref.shape[0]):
        _hgrn2_chunk(lc, sb, q_ref.at[b], f_ref.at[b], i_ref.at[b], g_ref.at[b], lb_ref, normw_ref,
                     y_ref.at[b], state_ref.at[b])


def _hgrn2_chunk(lc, sb, q_ref, f_ref, i_ref, g_ref, lb_ref, normw_ref, y_ref, state_ref):
    nsb = lc // sb
    lb = lb_ref[...]
    f = lb + (1.0 - lb) * jax.nn.sigmoid(f_ref[...])
    logf = jnp.log(f)
    kk = 1.0 - f
    q = _silu(q_ref[...])
    v = i_ref[...]
    v_bf = v.astype(BF16)

    row = lax.broadcasted_iota(jnp.int32, (lc, lc), 0)
    col = lax.broadcasted_iota(jnp.int32, (lc, lc), 1)
    tri_bf = (row >= col).astype(BF16)
    gcum = _dot_exact_rhs(tri_bf, logf)
    g_end = gcum[lc - 1:lc, :]
    k_end_bf = (kk * jnp.exp(g_end - gcum)).astype(BF16)
    state_decay = jnp.exp(g_end)

    q_inter_bf = (q * jnp.exp(gcum)).astype(BF16)
    key_row = lax.broadcasted_iota(jnp.int32, kk.shape, 0)
    qs_bf, k_ref_bf = [], []
    for ib in range(nsb):
        r0, r1 = ib * sb, (ib + 1) * sb
        if ib == 0:
            gref = jnp.zeros_like(g_end)
        else:
            gref = gcum[r0 - 1:r0, :]
        qs_bf.append((q[r0:r1, :] * jnp.exp(gcum[r0:r1, :] - gref)).astype(BF16))
        k_ref = kk * jnp.exp(gref - gcum)
        if r1 < lc:
            k_ref = jnp.where(key_row < r1, k_ref, 0.0)
        k_ref_bf.append(k_ref.astype(BF16))

    causal = row >= col
    heads = [(h * HG_DIM, (h + 1) * HG_DIM) for h in range(HG_HEADS)]
    states = [state_ref[lo:hi, :] for lo, hi in heads]
    o_inter = [_dot_nt(q_inter_bf[:, lo:hi], st.astype(BF16)) for (lo, hi), st in zip(heads, states)]
    for (lo, hi), st in zip(heads, states):
        state_ref[lo:hi, :] = (st * state_decay[:, lo:hi]
                               + _dot(v[:, lo:hi].T.astype(BF16), k_end_bf[:, lo:hi]))
    atts = [jnp.concatenate([_dot_nt(qs_bf[ib][:, lo:hi], k_ref_bf[ib][:, lo:hi])
                             for ib in range(nsb)], axis=0) for lo, hi in heads]
    atts = [jnp.where(causal, att, 0.0).astype(BF16) for att in atts]
    outs = [_dot(att, v_bf[:, lo:hi]) + oi for att, oi, (lo, hi) in zip(atts, o_inter, heads)]
    for o, (lo, hi) in zip(outs, heads):
        gate = _silu(g_ref[:, lo:hi])
        y_ref[:, lo:hi] = (_rmsnorm_rows(o, normw_ref[:, lo:hi]) * gate).astype(y_ref.dtype)


def _hgrn2_call(q, f, i, g, lb, normw, bsz, lc, sb):
    t, kd = q.shape
    seq = t // bsz
    tok = pl.BlockSpec((bsz, lc, kd), lambda c: (0, c, 0))
    const = lambda c: (0, 0)
    by_row = lambda a: a.reshape(bsz, seq, kd)
    y = pl.pallas_call(
        functools.partial(_hgrn2_kernel, lc, sb),
        grid=(seq // lc,),
        in_specs=[tok, tok, tok, tok, pl.BlockSpec(lb.shape, const), pl.BlockSpec(normw.shape, const)],
        out_specs=tok,
        out_shape=jax.ShapeDtypeStruct((bsz, seq, kd), BF16),
        scratch_shapes=[pltpu.VMEM((bsz, HG_HEADS * HG_DIM, HG_DIM), F32)],
        compiler_params=_params(("arbitrary",)),
        name="hgrn2",
    )(by_row(q), by_row(f), by_row(i), by_row(g), lb, normw)
    return y.reshape(t, kd)


def _outproj_kernel(x_ref, yssd_ref, yhg_ref, wout_ref, lnw_ref, wq_ref, keys_ref,
                    h_ref, hnt_ref, st_ref):
    inner = yssd_ref.shape[1]
    h = x_ref[...] + _dot(yssd_ref[...], wout_ref[:inner, :]) + _dot(yhg_ref[...], wout_ref[inner:, :])
    h_ref[...] = h
    hn_t = _rmsnorm_rows(h, lnw_ref[...]).T.astype(BF16)
    hnt_ref[...] = hn_t
    q_t = _dot(wq_ref[...], hn_t)
    n_hp = keys_ref.shape[0]
    half = keys_ref.shape[2]
    for hp in range(n_hp):
        st_ref[hp] = _dot(keys_ref[hp], q_t[hp * half:(hp + 1) * half, :].astype(BF16))


def _outproj_call(x2, y_ssd, y_hg, w_out, lnw, wq, keys, tm):
    t, d = x2.shape
    n_hp, n_keys, half = keys.shape
    const2 = lambda i: (0, 0)
    const3 = lambda i: (0, 0, 0)
    one = pl.Buffered(1)
    return pl.pallas_call(
        _outproj_kernel,
        grid=(t // tm,),
        in_specs=[
            pl.BlockSpec((tm, d), lambda i: (i, 0)),
            pl.BlockSpec((tm, y_ssd.shape[1]), lambda i: (i, 0)),
            pl.BlockSpec((tm, y_hg.shape[1]), lambda i: (i, 0)),
            pl.BlockSpec(w_out.shape, const2, pipeline_mode=one),
            pl.BlockSpec(lnw.shape, const2),
            pl.BlockSpec(wq.shape, const2, pipeline_mode=one),
            pl.BlockSpec(keys.shape, const3, pipeline_mode=one),
        ],
        out_specs=[
            pl.BlockSpec((tm, d), lambda i: (i, 0)),
            pl.BlockSpec((d, tm), lambda i: (0, i)),
            pl.BlockSpec((n_hp, n_keys, tm), lambda i: (0, 0, i)),
        ],
        out_shape=[
            jax.ShapeDtypeStruct((t, d), F32),
            jax.ShapeDtypeStruct((d, t), BF16),
            jax.ShapeDtypeStruct((n_hp, n_keys, t), F32),
        ],
        compiler_params=_params(("parallel",)),
        name="outproj",
    )(x2, y_ssd, y_hg, w_out, lnw, wq, keys)


NO_RANK = 255.0


def _extract_top(v, n_take, out_ref, want_rank):
    rank = jnp.full(v.shape, NO_RANK, F32) if want_rank else None
    for r in range(n_take):
        m = jnp.max(v, axis=0, keepdims=True)
        out_ref[r:r + 1, :] = m
        if want_rank or r + 1 < n_take:
            hit = v == m
            if want_rank:
                rank = jnp.where(hit, np.float32(r), rank)
            if r + 1 < n_take:
                v = jnp.where(hit, -jnp.inf, v)
    return rank


def _topk_kernel(st_ref, r2_ref, e2_ref, n1_ref, a1_ref, *list_refs):
    n_keys = st_ref.shape[1]
    tk = st_ref.shape[2]
    k = PEER_TOPK
    k_iota = lax.broadcasted_iota(jnp.int32, (k, LANES), 0)

    def one_head(ls, h, t1_ref, t2_ref, tc_ref):
        s1 = st_ref[2 * h, :, ls]
        s2 = st_ref[2 * h + 1, :, ls]
        _extract_top(s1, k, t1_ref, False)
        rank2 = _extract_top(s2, k, t2_ref, True)
        t1 = t1_ref[...]
        t2 = t2_ref[...]
        tiles = [t1[0:1, :] + t2]
        for a in range(1, k // 2):
            nb = k // (a + 1)
            tiles.append(jnp.where(k_iota[:SUBLANES] < nb, t1[a:a + 1, :] + t2[:SUBLANES, :],
                                   -jnp.inf))
        tiles.append(t1[k // 2:, :] + t2[0:1, :])
        cand = jnp.concatenate(tiles, axis=0)
        _extract_top(cand, k, tc_ref, False)
        top = tc_ref[...]
        cmax = top[0:1, :]
        z = jnp.sum(jnp.exp(top - cmax), axis=0, keepdims=True)
        tau = top[k - 1:k, :]
        passed = jnp.where(jnp.logical_and(cand >= tau, cand > -jnp.inf), 1.0, 0.0)
        counts = [jnp.sum(passed[0:k, :], axis=0, keepdims=True)]
        for a in range(1, k // 2):
            r0 = k + (a - 1) * SUBLANES
            counts.append(jnp.sum(passed[r0:r0 + SUBLANES, :], axis=0, keepdims=True))
        r0 = k + (k // 2 - 1) * SUBLANES
        counts += [passed[r0 + a:r0 + a + 1, :] for a in range(k - k // 2)]
        n_row = jnp.zeros((n_keys, LANES), F32)
        for a in range(k):
            n_row = jnp.where(s1 == t1[a:a + 1, :], counts[a], n_row)
        r2_ref[h, :, ls] = rank2
        e2_ref[h, :, ls] = jnp.exp(s2 - t2[0:1, :])
        n1_ref[h, :, ls] = n_row
        a1_ref[h, :, ls] = jnp.exp(s1 - t1[0:1, :]) / z

    def strip(s, carry):
        ls = pl.ds(pl.multiple_of(s * LANES, LANES), LANES)
        for h in range(PEER_HEADS):
            one_head(ls, h, *list_refs[3 * h:3 * h + 3])
        return carry

    lax.fori_loop(0, tk // LANES, strip, 0)


def _topk_call(st, tk):
    n_hp, n_keys, t = st.shape
    nh = n_hp // 2
    blk = pl.BlockSpec((nh, n_keys, tk), lambda i: (0, 0, i))
    per_key = jax.ShapeDtypeStruct((nh, n_keys, t), F32)
    return pl.pallas_call(
        _topk_kernel,
        grid=(t // tk,),
        in_specs=[pl.BlockSpec((n_hp, n_keys, tk), lambda i: (0, 0, i))],
        out_specs=[blk, blk, blk, blk],
        out_shape=[per_key, per_key, per_key, per_key],
        scratch_shapes=[pltpu.VMEM((PEER_TOPK, LANES), F32)] * (3 * PEER_HEADS),
        compiler_params=_params(("parallel",)),
        name="peer_topk",
    )(st)


BF16_ROWS = 2 * SUBLANES


def _peer_kernel(ib, hnt_ref, r2_ref, e2_ref, n1_ref, a1_ref, u_ref, vt_ref, h_ref, lnf_ref,
                 out_ref, acc_ref, s_ref, hid_ref, tab_ref):
    step = pl.program_id(1)
    n_keys = r2_ref.shape[1]
    tt = hnt_ref.shape[1]
    sqrt_half = np.float32(np.sqrt(0.5))

    @pl.when(step == 0)
    def _():
        acc_ref[...] = jnp.zeros(acc_ref.shape, F32)
        for h in range(PEER_HEADS):
            tab_ref[2 * h, :n_keys, :tt] = r2_ref[h].astype(BF16)
            tab_ref[2 * h + 1, :n_keys, :tt] = e2_ref[h].astype(BF16)

    s_ref[:, :tt] = _dot(u_ref[...], hnt_ref[...])

    def bcast_rows(ref, h, ii, ls):
        return jnp.broadcast_to(ref[h, ii:ii + 1, ls], (BF16_ROWS, LANES)).astype(BF16)

    zero = jnp.zeros((BF16_ROWS, LANES), BF16)
    for ii in range(ib):
        for s in range(tt // LANES):
            ls = slice(s * LANES, (s + 1) * LANES)
            n_rows = [bcast_rows(n1_ref, h, ii, ls) for h in range(PEER_HEADS)]
            a_rows = [bcast_rows(a1_ref, h, ii, ls) for h in range(PEER_HEADS)]
            for jt in range(n_keys // BF16_ROWS):
                kr = slice(jt * BF16_ROWS, (jt + 1) * BF16_ROWS)
                gate = None
                for h in range(PEER_HEADS):
                    term = jnp.where(tab_ref[2 * h, kr, ls] < n_rows[h],
                                     tab_ref[2 * h + 1, kr, ls], zero)
                    term = term * a_rows[h]
                    gate = term if gate is None else gate + term
                er = slice(ii * n_keys + jt * BF16_ROWS, ii * n_keys + (jt + 1) * BF16_ROWS)
                pre = s_ref[er, ls]
                act = 0.5 * pre * (1.0 + lax.erf(pre * sqrt_half))
                hid_ref[er, ls] = act.astype(BF16) * gate

    acc_ref[:, :tt] += _dot(vt_ref[...], hid_ref[...])

    @pl.when(step == pl.num_programs(1) - 1)
    def _():
        total = h_ref[...] + acc_ref[:, :tt].T
        out_ref[...] = _rmsnorm_rows(total, lnf_ref[...])


def _peer_call(hnt, r2, e2, n1, a1, u_bf, vt_bf, h, lnf, tt, ib):
    d, t = hnt.shape
    nh, n_keys, _ = r2.shape
    n_exp = u_bf.shape[0]
    eb = ib * n_keys
    full = pl.BlockSpec((nh, n_keys, tt), lambda i, j: (0, 0, i))
    rows = pl.BlockSpec((nh, ib, tt), lambda i, j: (0, j, i))
    return pl.pallas_call(
        functools.partial(_peer_kernel, ib),
        grid=(t // tt, n_exp // eb),
        in_specs=[
            pl.BlockSpec((d, tt), lambda i, j: (0, i)),
            full, full, rows, rows,
            pl.BlockSpec((eb, d), lambda i, j: (j, 0)),
            pl.BlockSpec((d, eb), lambda i, j: (0, j)),
            pl.BlockSpec((tt, d), lambda i, j: (i, 0)),
            pl.BlockSpec((1, d), lambda i, j: (0, 0)),
        ],
        out_specs=pl.BlockSpec((tt, d), lambda i, j: (i, 0)),
        out_shape=jax.ShapeDtypeStruct((t, d), F32),
        scratch_shapes=[
            pltpu.VMEM((d, tt + LANES), F32),
            pltpu.VMEM((eb, tt + LANES), F32),
            pltpu.VMEM((eb, tt), BF16),
            pltpu.VMEM((2 * nh, n_keys + BF16_ROWS, tt + LANES), BF16),
        ],
        compiler_params=_params(("parallel", "arbitrary")),
        name="peer_experts",
    )(hnt, r2, e2, n1, a1, u_bf, vt_bf, h, lnf)


def _layer(h2, bsz, lb, ln_mix_w, w_in, conv_w, conv_b, dt_bias, a_log, d_skip, ssd_norm_w,
           hg_norm_w, w_out, ln_ffn_w, w_query, sub_keys, u_table, v_table, ln_out_w,
           tm, lc_ssd, lc_hg, sb_hg, tk, tt, ib):
    t, d = h2.shape
    inner = SSD_HEADS * SSD_HEAD_DIM
    conv_ch = inner + 2 * SSD_GROUPS * SSD_STATE
    hgk = HG_HEADS * HG_DIM

    sizes = (inner, conv_ch, SSD_HEADS, hgk, hgk, hgk, hgk)
    offs = np.concatenate([[0], np.cumsum(sizes)])
    wz, wxbc, wdt, wq_, wf, wi, wg = [w_in[:, offs[n]:offs[n + 1]] for n in range(7)]
    wdt_pad = jnp.pad(wdt, ((0, 0), (0, LANES - SSD_HEADS)))
    w_pad = jnp.concatenate([wz, wxbc, wq_, wf, wi, wg, wdt_pad], axis=1).astype(BF16)
    widths = (inner, conv_ch, hgk, hgk, hgk, hgk, LANES)
    starts = np.concatenate([[0], np.cumsum(widths)])[:-1]
    col_slices = tuple((int(s), int(w)) for s, w in zip(starts, widths))

    z, xbc, q_raw, f_raw, i_raw, g_raw, dt_raw = _inproj_call(
        h2, ln_mix_w.reshape(1, d), w_pad, col_slices, tm)

    pad_h = (0, LANES - SSD_HEADS)
    dtb = jnp.pad(dt_bias.astype(F32), pad_h).reshape(1, LANES)
    ahead = jnp.pad(-jnp.exp(a_log.astype(F32)), pad_h).reshape(1, LANES)
    dskip_x = jnp.repeat(d_skip.astype(F32), SSD_HEAD_DIM).reshape(1, inner)
    expand = (jnp.arange(LANES)[:, None] == (jnp.arange(inner)[None, :] // SSD_HEAD_DIM)).astype(BF16)
    y_ssd = _ssd_call(z, xbc, dt_raw, conv_w.astype(F32), conv_b.reshape(1, conv_ch).astype(F32),
                      dtb, ahead, dskip_x, ssd_norm_w.reshape(1, inner), expand, bsz, lc_ssd)

    y_hg = _hgrn2_call(q_raw, f_raw, i_raw, g_raw, lb.reshape(1, hgk), hg_norm_w.reshape(1, hgk),
                       bsz, lc_hg, sb_hg)

    n_hp = PEER_HEADS * 2
    half = sub_keys.shape[-1]
    h_res, hn_t, st = _outproj_call(h2, y_ssd, y_hg, w_out.astype(BF16), ln_ffn_w.reshape(1, d),
                                    w_query.T.astype(BF16),
                                    sub_keys.reshape(n_hp, PEER_KEYS, half).astype(BF16), tm)

    r2, e2, n1, a1 = _topk_call(st, tk)

    return _peer_call(hn_t, r2, e2, n1, a1, u_table.astype(BF16), v_table.T.astype(BF16),
                      h_res, ln_out_w.reshape(1, d), tt, ib)


def kernel(x, ln_mix_w, w_in, conv_w, conv_b, dt_bias, a_log, d_skip, ssd_norm_w, lb_logits,
           hg_norm_w, w_out, ln_ffn_w, w_query, sub_keys, u_table, v_table, ln_final_w):
    bsz, seqlen, d = x.shape
    depth = w_in.shape[0]
    assert depth == 1, "the final RMSNorm is fused into the (single) layer's expert kernel"
    lb_all = jnp.cumsum(jax.nn.softmax(lb_logits.astype(F32), axis=0), axis=0)
    t = bsz * seqlen
    tm = min(512, t)
    lc = min(128, seqlen)
    out = _layer(x.reshape(t, d), bsz, lb_all[0], ln_mix_w[0], w_in[0], conv_w[0], conv_b[0],
                 dt_bias[0], a_log[0], d_skip[0], ssd_norm_w[0], hg_norm_w[0], w_out[0],
                 ln_ffn_w[0], w_query[0], sub_keys[0], u_table[0], v_table[0], ln_final_w,
                 tm=tm, lc_ssd=lc, lc_hg=lc, sb_hg=32, tk=min(512, t),
                 tt=min(512, t), ib=2 * SUBLANES)
    return out.reshape(bsz, seqlen, d)
```

```python
import functools

import numpy as np
import jax
import jax.numpy as jnp
from jax import lax
from jax.experimental import pallas as pl
from jax.experimental.pallas import tpu as pltpu

F32 = jnp.float32
BF16 = jnp.bfloat16
EPS = 1e-6

SSD_HEADS = 16
SSD_HEAD_DIM = 64
SSD_GROUPS = 4
SSD_STATE = 128
SSD_CONV = 4
HG_HEADS = 8
HG_DIM = 128
PEER_HEADS = 8
PEER_KEYS = 128
PEER_TOPK = 16

LANES = 128
SUBLANES = 8
VMEM_LIMIT = 56 * 1024 * 1024

CONV_PAD = SUBLANES


def _params(sem):
    return pltpu.CompilerParams(dimension_semantics=sem, vmem_limit_bytes=VMEM_LIMIT)


def _split3(x):
    hi = x.astype(BF16)
    r1 = x - hi.astype(F32)
    mid = r1.astype(BF16)
    lo = (r1 - mid.astype(F32)).astype(BF16)
    return hi, mid, lo


def _split2(x):
    hi = x.astype(BF16)
    lo = (x - hi.astype(F32)).astype(BF16)
    return hi, lo


def _dot(a, b):
    return jnp.dot(a, b, preferred_element_type=F32)


def _dot_nt(a, b):
    return lax.dot_general(a, b, (((1,), (1,)), ((), ())), preferred_element_type=F32)


def _dot_exact_rhs(m01, x):
    hi, mid, lo = _split3(x)
    return _dot(m01, hi) + _dot(m01, mid) + _dot(m01, lo)


def _dot_exact_lhs(x, m01):
    hi, mid, lo = _split3(x)
    return _dot(hi, m01) + _dot(mid, m01) + _dot(lo, m01)


def _silu(x):
    return x * jax.nn.sigmoid(x)


def _softplus(x):
    return jnp.maximum(x, 0.0) + jnp.log1p(jnp.exp(-jnp.abs(x)))


def _rmsnorm_rows(x, w):
    ms = jnp.mean(x * x, axis=-1, keepdims=True)
    return x * lax.rsqrt(ms + EPS) * w


def _inproj_kernel(col_slices, x_ref, lnw_ref, w_ref, *out_refs):
    hn = _rmsnorm_rows(x_ref[...], lnw_ref[...]).astype(BF16)
    for ref, (off, width) in zip(out_refs, col_slices):
        ref[...] = _dot(hn, w_ref[:, off:off + width]).astype(ref.dtype)


def _inproj_call(x2, lnw, w_pad, col_slices, tm):
    t, d = x2.shape
    out_shape = [jax.ShapeDtypeStruct((t, width), F32) for _, width in col_slices]
    out_specs = [pl.BlockSpec((tm, width), lambda i: (i, 0)) for _, width in col_slices]
    return pl.pallas_call(
        functools.partial(_inproj_kernel, col_slices),
        grid=(t // tm,),
        in_specs=[
            pl.BlockSpec((tm, d), lambda i: (i, 0)),
            pl.BlockSpec((1, d), lambda i: (0, 0)),
            pl.BlockSpec(w_pad.shape, lambda i: (0, 0), pipeline_mode=pl.Buffered(1)),
        ],
        out_specs=out_specs,
        out_shape=out_shape,
        compiler_params=_params(("parallel",)),
        name="inproj",
    )(x2, lnw, w_pad)


def _ssd_chunk(lc, z_ref, xbc_ref, dt_ref, convw_ref, convb_ref, dtb_ref, ahead_ref, dskip_ref,
               normw_ref, expand_ref, y_ref, xpad_ref, state_ref, ybuf_ref):
    inner = SSD_HEADS * SSD_HEAD_DIM
    gn = SSD_GROUPS * SSD_STATE
    pair_w = 2 * SSD_HEAD_DIM

    xpad_ref[CONV_PAD:CONV_PAD + lc, :] = xbc_ref[...]
    conv = convb_ref[...]
    for k in range(SSD_CONV):
        start = CONV_PAD - (SSD_CONV - 1) + k
        conv = conv + convw_ref[k:k + 1, :] * xpad_ref[start:start + lc, :]
    xpad_ref[0:CONV_PAD, :] = xpad_ref[lc:lc + CONV_PAD, :]
    xbc = _silu(conv)
    xs = xbc[:, :inner]

    dt = _softplus(dt_ref[...] + dtb_ref[...])
    a = dt * ahead_ref[...]

    row = lax.broadcasted_iota(jnp.int32, (lc, lc), 0)
    col = lax.broadcasted_iota(jnp.int32, (lc, lc), 1)
    tril = row >= col
    tri_bf = tril.astype(BF16)
    acum = _dot_exact_rhs(tri_bf, a)
    acum_t = acum.T
    a_last = acum[lc - 1:lc, :]

    expand = expand_ref[...]
    dt_x = _dot_exact_lhs(dt, expand)
    from_start_x = _dot_exact_lhs(jnp.exp(acum), expand)
    to_end_x = _dot_exact_lhs(jnp.exp(a_last - acum), expand)
    chunk_decay = jnp.exp(a_last)
    xdt = xs * dt_x
    xw = xdt * to_end_x

    lane = lax.broadcasted_iota(jnp.int32, (lc, pair_w), 1)
    first_half = lane < SSD_HEAD_DIM

    heads_per_group = SSD_HEADS // SSD_GROUPS
    for g in range(SSD_GROUPS):
        bm = xbc[:, inner + g * SSD_STATE: inner + (g + 1) * SSD_STATE].astype(BF16)
        cm = xbc[:, inner + gn + g * SSD_STATE: inner + gn + (g + 1) * SSD_STATE].astype(BF16)
        cb = _dot_nt(cm, bm)
        for pr in range(heads_per_group // 2):
            pidx = g * (heads_per_group // 2) + pr
            lo, hi = pidx * pair_w, (pidx + 1) * pair_w
            xdt_p = xdt[:, lo:hi]
            y_pair = None
            for sub in range(2):
                h = 2 * pidx + sub
                diff = acum[:, h:h + 1] - acum_t[h:h + 1, :]
                decay = jnp.where(tril, jnp.exp(jnp.where(tril, diff, 0.0)), 0.0)
                m_h = (cb * decay).astype(BF16)
                sel = first_half if sub == 0 else jnp.logical_not(first_half)
                contrib = _dot(m_h, jnp.where(sel, xdt_p, 0.0).astype(BF16))
                y_pair = contrib if y_pair is None else y_pair + contrib
            st = state_ref[lo:hi, :]
            y_off = _dot_nt(cm, st.astype(BF16)) * from_start_x[:, lo:hi]
            ybuf_ref[:, lo:hi] = y_pair + y_off + xs[:, lo:hi] * dskip_ref[:, lo:hi]
            r_idx = lax.broadcasted_iota(jnp.int32, (pair_w, SSD_STATE), 0)
            dec = jnp.where(r_idx < SSD_HEAD_DIM, chunk_decay[:, 2 * pidx:2 * pidx + 1],
                            chunk_decay[:, 2 * pidx + 1:2 * pidx + 2])
            state_ref[lo:hi, :] = st * dec + _dot(xw[:, lo:hi].T.astype(BF16), bm)

    y = ybuf_ref[...] * _silu(z_ref[...])
    y_ref[...] = _rmsnorm_rows(y, normw_ref[...]).astype(y_ref.dtype)


def _mixer_kernel(lc, sb, z_ref, xbc_ref, dt_ref, q_ref, f_ref, i_ref, g_ref, convw_ref, convb_ref,
                  dtb_ref, ahead_ref, dskip_ref, ssd_normw_ref, expand_ref, lb_ref, hg_normw_ref,
                  yssd_ref, yhg_ref, xpad_ref, ssd_state_ref, ybuf_ref, hg_state_ref):
    @pl.when(pl.program_id(0) == 0)
    def _():
        xpad_ref[:, 0:CONV_PAD, :] = jnp.zeros(
            (xpad_ref.shape[0], CONV_PAD, xpad_ref.shape[2]), F32)
        ssd_state_ref[...] = jnp.zeros(ssd_state_ref.shape, F32)
        hg_state_ref[...] = jnp.zeros(hg_state_ref.shape, F32)

    for b in range(z_ref.shape[0]):
        _ssd_chunk(lc, z_ref.at[b], xbc_ref.at[b], dt_ref.at[b], convw_ref, convb_ref, dtb_ref,
                   ahead_ref, dskip_ref, ssd_normw_ref, expand_ref, yssd_ref.at[b], xpad_ref.at[b],
                   ssd_state_ref.at[b], ybuf_ref.at[b])
        _hgrn2_chunk(lc, sb, q_ref.at[b], f_ref.at[b], i_ref.at[b], g_ref.at[b], lb_ref,
                     hg_normw_ref, yhg_ref.at[b], hg_state_ref.at[b])


def _mixer_call(z, xbc, dtp, q, f, i, g, convw, convb, dtb, ahead, dskip_x, ssd_normw, expand,
                lb, hg_normw, bsz, lc, sb):
    t, inner = z.shape
    cw = xbc.shape[1]
    kd = q.shape[1]
    seq = t // bsz
    tok = lambda width: pl.BlockSpec((bsz, lc, width), lambda c: (0, c, 0))
    const = lambda a: pl.BlockSpec(a.shape, lambda c: (0, 0))
    by_row = lambda a: a.reshape(bsz, seq, a.shape[1])
    y_ssd, y_hg = pl.pallas_call(
        functools.partial(_mixer_kernel, lc, sb),
        grid=(seq // lc,),
        in_specs=[tok(inner), tok(cw), tok(LANES), tok(kd), tok(kd), tok(kd), tok(kd),
                  const(convw), const(convb), const(dtb), const(ahead), const(dskip_x),
                  const(ssd_normw), const(expand), const(lb), const(hg_normw)],
        out_specs=[tok(inner), tok(kd)],
        out_shape=[jax.ShapeDtypeStruct((bsz, seq, inner), BF16),
                   jax.ShapeDtypeStruct((bsz, seq, kd), BF16)],
        scratch_shapes=[
            pltpu.VMEM((bsz, lc + CONV_PAD, cw), F32),
            pltpu.VMEM((bsz, inner, SSD_STATE), F32),
            pltpu.VMEM((bsz, lc, inner), F32),
            pltpu.VMEM((bsz, HG_HEADS * HG_DIM, HG_DIM), F32),
        ],
        compiler_params=_params(("arbitrary",)),
        name="mixers",
    )(by_row(z), by_row(xbc), by_row(dtp), by_row(q), by_row(f), by_row(i), by_row(g),
      convw, convb, dtb, ahead, dskip_x, ssd_normw, expand, lb, hg_normw)
    return y_ssd.reshape(t, inner), y_hg.reshape(t, kd)


def _hgrn2_chunk(lc, sb, q_ref, f_ref, i_ref, g_ref, lb_ref, normw_ref, y_ref, state_ref):
    nsb = lc // sb
    lb = lb_ref[...]
    f = lb + (1.0 - lb) * jax.nn.sigmoid(f_ref[...])
    logf = jnp.log(f)
    kk = 1.0 - f
    q = _silu(q_ref[...])
    v = i_ref[...]
    v_bf = v.astype(BF16)

    row = lax.broadcasted_iota(jnp.int32, (lc, lc), 0)
    col = lax.broadcasted_iota(jnp.int32, (lc, lc), 1)
    tri_bf = (row >= col).astype(BF16)
    gcum = _dot_exact_rhs(tri_bf, logf)
    g_end = gcum[lc - 1:lc, :]
    k_end_bf = (kk * jnp.exp(g_end - gcum)).astype(BF16)
    state_decay = jnp.exp(g_end)

    q_inter_bf = (q * jnp.exp(gcum)).astype(BF16)
    key_row = lax.broadcasted_iota(jnp.int32, kk.shape, 0)
    qs_bf, k_ref_bf = [], []
    for ib in range(nsb):
        r0, r1 = ib * sb, (ib + 1) * sb
        if ib == 0:
            gref = jnp.zeros_like(g_end)
        else:
            gref = gcum[r0 - 1:r0, :]
        qs_bf.append((q[r0:r1, :] * jnp.exp(gcum[r0:r1, :] - gref)).astype(BF16))
        k_ref = kk * jnp.exp(gref - gcum)
        if r1 < lc:
            k_ref = jnp.where(key_row < r1, k_ref, 0.0)
        k_ref_bf.append(k_ref.astype(BF16))

    causal = row >= col
    heads = [(h * HG_DIM, (h + 1) * HG_DIM) for h in range(HG_HEADS)]
    states = [state_ref[lo:hi, :] for lo, hi in heads]
    o_inter = [_dot_nt(q_inter_bf[:, lo:hi], st.astype(BF16)) for (lo, hi), st in zip(heads, states)]
    for (lo, hi), st in zip(heads, states):
        state_ref[lo:hi, :] = (st * state_decay[:, lo:hi]
                               + _dot(v[:, lo:hi].T.astype(BF16), k_end_bf[:, lo:hi]))
    atts = [jnp.concatenate([_dot_nt(qs_bf[ib][:, lo:hi], k_ref_bf[ib][:, lo:hi])
                             for ib in range(nsb)], axis=0) for lo, hi in heads]
    atts = [jnp.where(causal, att, 0.0).astype(BF16) for att in atts]
    outs = [_dot(att, v_bf[:, lo:hi]) + oi for att, oi, (lo, hi) in zip(atts, o_inter, heads)]
    for o, (lo, hi) in zip(outs, heads):
        gate = _silu(g_ref[:, lo:hi])
        y_ref[:, lo:hi] = (_rmsnorm_rows(o, normw_ref[:, lo:hi]) * gate).astype(y_ref.dtype)


def _outproj_kernel(x_ref, yssd_ref, yhg_ref, wout_ref, lnw_ref, wq_ref, keys_ref,
                    h_ref, hnt_ref, st_ref):
    inner = yssd_ref.shape[1]
    h = x_ref[...] + _dot(yssd_ref[...], wout_ref[:inner, :]) + _dot(yhg_ref[...], wout_ref[inner:, :])
    h_ref[...] = h
    hn_t = _rmsnorm_rows(h, lnw_ref[...]).T.astype(BF16)
    hnt_ref[...] = hn_t
    q_t = _dot(wq_ref[...], hn_t)
    n_hp = keys_ref.shape[0]
    half = keys_ref.shape[2]
    for hp in range(n_hp):
        st_ref[hp] = _dot(keys_ref[hp], q_t[hp * half:(hp + 1) * half, :].astype(BF16))


def _outproj_call(x2, y_ssd, y_hg, w_out, lnw, wq, keys, tm):
    t, d = x2.shape
    n_hp, n_keys, half = keys.shape
    const2 = lambda i: (0, 0)
    const3 = lambda i: (0, 0, 0)
    one = pl.Buffered(1)
    return pl.pallas_call(
        _outproj_kernel,
        grid=(t // tm,),
        in_specs=[
            pl.BlockSpec((tm, d), lambda i: (i, 0)),
            pl.BlockSpec((tm, y_ssd.shape[1]), lambda i: (i, 0)),
            pl.BlockSpec((tm, y_hg.shape[1]), lambda i: (i, 0)),
            pl.BlockSpec(w_out.shape, const2, pipeline_mode=one),
            pl.BlockSpec(lnw.shape, const2),
            pl.BlockSpec(wq.shape, const2, pipeline_mode=one),
            pl.BlockSpec(keys.shape, const3, pipeline_mode=one),
        ],
        out_specs=[
            pl.BlockSpec((tm, d), lambda i: (i, 0)),
            pl.BlockSpec((d, tm), lambda i: (0, i)),
            pl.BlockSpec((n_hp, n_keys, tm), lambda i: (0, 0, i)),
        ],
        out_shape=[
            jax.ShapeDtypeStruct((t, d), F32),
            jax.ShapeDtypeStruct((d, t), BF16),
            jax.ShapeDtypeStruct((n_hp, n_keys, t), F32),
        ],
        compiler_params=_params(("parallel",)),
        name="outproj",
    )(x2, y_ssd, y_hg, w_out, lnw, wq, keys)


NO_RANK = 255.0


def _extract_top(v, n_take, out_ref, want_rank):
    rank = jnp.full(v.shape, NO_RANK, F32) if want_rank else None
    for r in range(n_take):
        m = jnp.max(v, axis=0, keepdims=True)
        out_ref[r:r + 1, :] = m
        if want_rank or r + 1 < n_take:
            hit = v == m
            if want_rank:
                rank = jnp.where(hit, np.float32(r), rank)
            if r + 1 < n_take:
                v = jnp.where(hit, -jnp.inf, v)
    return rank


def _topk_kernel(st_ref, r2_ref, e2_ref, n1_ref, a1_ref, *list_refs):
    n_keys = st_ref.shape[1]
    tk = st_ref.shape[2]
    k = PEER_TOPK
    k_iota = lax.broadcasted_iota(jnp.int32, (k, LANES), 0)

    def one_head(ls, h, t1_ref, t2_ref, tc_ref):
        s1 = st_ref[2 * h, :, ls]
        s2 = st_ref[2 * h + 1, :, ls]
        _extract_top(s1, k, t1_ref, False)
        rank2 = _extract_top(s2, k, t2_ref, True)
        t1 = t1_ref[...]
        t2 = t2_ref[...]
        tiles = [t1[0:1, :] + t2]
        for a in range(1, k // 2):
            nb = k // (a + 1)
            tiles.append(jnp.where(k_iota[:SUBLANES] < nb, t1[a:a + 1, :] + t2[:SUBLANES, :],
                                   -jnp.inf))
        tiles.append(t1[k // 2:, :] + t2[0:1, :])
        cand = jnp.concatenate(tiles, axis=0)
        _extract_top(cand, k, tc_ref, False)
        top = tc_ref[...]
        cmax = top[0:1, :]
        z = jnp.sum(jnp.exp(top - cmax), axis=0, keepdims=True)
        tau = top[k - 1:k, :]
        passed = jnp.where(jnp.logical_and(cand >= tau, cand > -jnp.inf), 1.0, 0.0)
        counts = [jnp.sum(passed[0:k, :], axis=0, keepdims=True)]
        for a in range(1, k // 2):
            r0 = k + (a - 1) * SUBLANES
            counts.append(jnp.sum(passed[r0:r0 + SUBLANES, :], axis=0, keepdims=True))
        r0 = k + (k // 2 - 1) * SUBLANES
        counts += [passed[r0 + a:r0 + a + 1, :] for a in range(k - k // 2)]
        n_row = jnp.zeros((n_keys, LANES), F32)
        for a in range(k):
            n_row = jnp.where(s1 == t1[a:a + 1, :], counts[a], n_row)
        r2_ref[h, :, ls] = rank2
        e2_ref[h, :, ls] = jnp.exp(s2 - t2[0:1, :])
        n1_ref[h, :, ls] = n_row
        a1_ref[h, :, ls] = jnp.exp(s1 - t1[0:1, :]) / z

    def strip(s, carry):
        ls = pl.ds(pl.multiple_of(s * LANES, LANES), LANES)
        for h in range(PEER_HEADS):
            one_head(ls, h, *list_refs[3 * h:3 * h + 3])
        return carry

    lax.fori_loop(0, tk // LANES, strip, 0)


def _topk_call(st, tk):
    n_hp, n_keys, t = st.shape
    nh = n_hp // 2
    blk = pl.BlockSpec((nh, n_keys, tk), lambda i: (0, 0, i))
    per_key = jax.ShapeDtypeStruct((nh, n_keys, t), F32)
    return pl.pallas_call(
        _topk_kernel,
        grid=(t // tk,),
        in_specs=[pl.BlockSpec((n_hp, n_keys, tk), lambda i: (0, 0, i))],
        out_specs=[blk, blk, blk, blk],
        out_shape=[per_key, per_key, per_key, per_key],
        scratch_shapes=[pltpu.VMEM((PEER_TOPK, LANES), F32)] * (3 * PEER_HEADS),
        compiler_params=_params(("parallel",)),
        name="peer_topk",
    )(st)


BF16_ROWS = 2 * SUBLANES


def _peer_kernel(ib, hnt_ref, r2_ref, e2_ref, n1_ref, a1_ref, u_ref, vt_ref, h_ref, lnf_ref,
                 out_ref, acc_ref, s_ref, hid_ref, tab_ref):
    step = pl.program_id(1)
    n_keys = r2_ref.shape[1]
    tt = hnt_ref.shape[1]
    sqrt_half = np.float32(np.sqrt(0.5))

    @pl.when(step == 0)
    def _():
        acc_ref[...] = jnp.zeros(acc_ref.shape, F32)
        for h in range(PEER_HEADS):
            tab_ref[2 * h, :n_keys, :tt] = r2_ref[h].astype(BF16)
            tab_ref[2 * h + 1, :n_keys, :tt] = e2_ref[h].astype(BF16)

    s_ref[:, :tt] = _dot(u_ref[...], hnt_ref[...])

    def bcast_rows(ref, h, ii, ls):
        return jnp.broadcast_to(ref[h, ii:ii + 1, ls], (BF16_ROWS, LANES)).astype(BF16)

    zero = jnp.zeros((BF16_ROWS, LANES), BF16)
    for ii in range(ib):
        for s in range(tt // LANES):
            ls = slice(s * LANES, (s + 1) * LANES)
            n_rows = [bcast_rows(n1_ref, h, ii, ls) for h in range(PEER_HEADS)]
            a_rows = [bcast_rows(a1_ref, h, ii, ls) for h in range(PEER_HEADS)]
            for jt in range(n_keys // BF16_ROWS):
                kr = slice(jt * BF16_ROWS, (jt + 1) * BF16_ROWS)
                gate = None
                for h in range(PEER_HEADS):
                    term = jnp.where(tab_ref[2 * h, kr, ls] < n_rows[h],
                                     tab_ref[2 * h + 1, kr, ls], zero)
                    term = term * a_rows[h]
                    gate = term if gate is None else gate + term
                er = slice(ii * n_keys + jt * BF16_ROWS, ii * n_keys + (jt + 1) * BF16_ROWS)
                pre = s_ref[er, ls]
                act = 0.5 * pre * (1.0 + lax.erf(pre * sqrt_half))
                hid_ref[er, ls] = act.astype(BF16) * gate

    acc_ref[:, :tt] += _dot(vt_ref[...], hid_ref[...])

    @pl.when(step == pl.num_programs(1) - 1)
    def _():
        total = h_ref[...] + acc_ref[:, :tt].T
        out_ref[...] = _rmsnorm_rows(total, lnf_ref[...])


def _peer_call(hnt, r2, e2, n1, a1, u_bf, vt_bf, h, lnf, tt, ib):
    d, t = hnt.shape
    nh, n_keys, _ = r2.shape
    n_exp = u_bf.shape[0]
    eb = ib * n_keys
    full = pl.BlockSpec((nh, n_keys, tt), lambda i, j: (0, 0, i))
    rows = pl.BlockSpec((nh, ib, tt), lambda i, j: (0, j, i))
    return pl.pallas_call(
        functools.partial(_peer_kernel, ib),
        grid=(t // tt, n_exp // eb),
        in_specs=[
            pl.BlockSpec((d, tt), lambda i, j: (0, i)),
            full, full, rows, rows,
            pl.BlockSpec((eb, d), lambda i, j: (j, 0)),
            pl.BlockSpec((d, eb), lambda i, j: (0, j)),
            pl.BlockSpec((tt, d), lambda i, j: (i, 0)),
            pl.BlockSpec((1, d), lambda i, j: (0, 0)),
        ],
        out_specs=pl.BlockSpec((tt, d), lambda i, j: (i, 0)),
        out_shape=jax.ShapeDtypeStruct((t, d), F32),
        scratch_shapes=[
            pltpu.VMEM((d, tt + LANES), F32),
            pltpu.VMEM((eb, tt + LANES), F32),
            pltpu.VMEM((eb, tt), BF16),
            pltpu.VMEM((2 * nh, n_keys + BF16_ROWS, tt + LANES), BF16),
        ],
        compiler_params=_params(("parallel", "arbitrary")),
        name="peer_experts",
    )(hnt, r2, e2, n1, a1, u_bf, vt_bf, h, lnf)


def _layer(h2, bsz, lb, ln_mix_w, w_in, conv_w, conv_b, dt_bias, a_log, d_skip, ssd_norm_w,
           hg_norm_w, w_out, ln_ffn_w, w_query, sub_keys, u_table, v_table, ln_out_w,
           tm, lc_ssd, lc_hg, sb_hg, tk, tt, ib):
    t, d = h2.shape
    inner = SSD_HEADS * SSD_HEAD_DIM
    conv_ch = inner + 2 * SSD_GROUPS * SSD_STATE
    hgk = HG_HEADS * HG_DIM

    sizes = (inner, conv_ch, SSD_HEADS, hgk, hgk, hgk, hgk)
    offs = np.concatenate([[0], np.cumsum(sizes)])
    wz, wxbc, wdt, wq_, wf, wi, wg = [w_in[:, offs[n]:offs[n + 1]] for n in range(7)]
    wdt_pad = jnp.pad(wdt, ((0, 0), (0, LANES - SSD_HEADS)))
    w_pad = jnp.concatenate([wz, wxbc, wq_, wf, wi, wg, wdt_pad], axis=1).astype(BF16)
    widths = (inner, conv_ch, hgk, hgk, hgk, hgk, LANES)
    starts = np.concatenate([[0], np.cumsum(widths)])[:-1]
    col_slices = tuple((int(s), int(w)) for s, w in zip(starts, widths))

    z, xbc, q_raw, f_raw, i_raw, g_raw, dt_raw = _inproj_call(
        h2, ln_mix_w.reshape(1, d), w_pad, col_slices, tm)

    pad_h = (0, LANES - SSD_HEADS)
    dtb = jnp.pad(dt_bias.astype(F32), pad_h).reshape(1, LANES)
    ahead = jnp.pad(-jnp.exp(a_log.astype(F32)), pad_h).reshape(1, LANES)
    dskip_x = jnp.repeat(d_skip.astype(F32), SSD_HEAD_DIM).reshape(1, inner)
    expand = (jnp.arange(LANES)[:, None] == (jnp.arange(inner)[None, :] // SSD_HEAD_DIM)).astype(BF16)
    assert lc_ssd == lc_hg
    y_ssd, y_hg = _mixer_call(
        z, xbc, dt_raw, q_raw, f_raw, i_raw, g_raw, conv_w.astype(F32),
        conv_b.reshape(1, conv_ch).astype(F32), dtb, ahead, dskip_x, ssd_norm_w.reshape(1, inner),
        expand, lb.reshape(1, hgk), hg_norm_w.reshape(1, hgk), bsz, lc_ssd, sb_hg)

    n_hp = PEER_HEADS * 2
    half = sub_keys.shape[-1]
    h_res, hn_t, st = _outproj_call(h2, y_ssd, y_hg, w_out.astype(BF16), ln_ffn_w.reshape(1, d),
                                    w_query.T.astype(BF16),
                                    sub_keys.reshape(n_hp, PEER_KEYS, half).astype(BF16), tm)

    r2, e2, n1, a1 = _topk_call(st, tk)

    return _peer_call(hn_t, r2, e2, n1, a1, u_table.astype(BF16), v_table.T.astype(BF16),
                      h_res, ln_out_w.reshape(1, d), tt, ib)


def kernel(x, ln_mix_w, w_in, conv_w, conv_b, dt_bias, a_log, d_skip, ssd_norm_w, lb_logits,
           hg_norm_w, w_out, ln_ffn_w, w_query, sub_keys, u_table, v_table, ln_final_w):
    bsz, seqlen, d = x.shape
    depth = w_in.shape[0]
    assert depth == 1, "the final RMSNorm is fused into the (single) layer's expert kernel"
    lb_all = jnp.cumsum(jax.nn.softmax(lb_logits.astype(F32), axis=0), axis=0)
    t = bsz * seqlen
    tm = min(512, t)
    lc = min(128, seqlen)
    out = _layer(x.reshape(t, d), bsz, lb_all[0], ln_mix_w[0], w_in[0], conv_w[0], conv_b[0],
                 dt_bias[0], a_log[0], d_skip[0], ssd_norm_w[0], hg_norm_w[0], w_out[0],
                 ln_ffn_w[0], w_query[0], sub_keys[0], u_table[0], v_table[0], ln_final_w,
                 tm=tm, lc_ssd=lc, lc_hg=lc, sb_hg=32, tk=min(512, t),
                 tt=min(512, t), ib=2 * SUBLANES)
    return out.reshape(bsz, seqlen, d)
```
